```python
import jax, jax.numpy as jnp
from jax import lax
import numpy as np

D_MODEL = 1024
BATCH = 8
SEQ = 8192
DEPTH = 4

CHUNK = 64
N_MIXERS = 2
MEM_LEN = 256
HEAD_DIM = 64
D_TOK = D_MODEL // 2
D_MEMH = D_MODEL // 4
D_MIX = D_TOK + D_MEMH
N_SB_HEADS = D_TOK // HEAD_DIM
N_MEM_HEADS = D_MEMH // HEAD_DIM
POOL_WINDOWS = (2, 4, 8, 16)
N_POOL_GROUPS = len(POOL_WINDOWS)
POOL_GROUP = D_TOK // N_POOL_GROUPS
D_FF = 2 * D_MODEL
Q_BLOCK = 128
EPS = 1e-6
N_A = (DEPTH + N_MIXERS - 1) // N_MIXERS
N_B = DEPTH // N_MIXERS

kernel_name = "hybrid_pool_stickbreak_memory_trunk"


def rmsnorm(x, g):
    xf = x.astype(jnp.float32)
    y = xf * lax.rsqrt(jnp.mean(xf * xf, axis=-1, keepdims=True) + EPS) * g.astype(jnp.float32)
    return y.astype(x.dtype)


def swiglu(h, w_gate, w_up, w_down):
    return (jax.nn.silu(h @ w_gate) * (h @ w_up)) @ w_down


def pool_mixer(u, pool_w, pool_scale):
    B, S, _ = u.shape
    uf = u.astype(jnp.float32)
    cs = jnp.cumsum(uf, axis=1)
    pos = jnp.arange(S)
    outs = []
    for gi, w in enumerate(POOL_WINDOWS):
        sl = slice(gi * POOL_GROUP, (gi + 1) * POOL_GROUP)
        c = cs[..., sl]
        lagged = jnp.pad(c, ((0, 0), (w, 0), (0, 0)))[:, :S]
        cnt = jnp.minimum(pos + 1, w).astype(jnp.float32)[None, :, None]
        outs.append((c - lagged) / cnt - uf[..., sl])
    d = jnp.stack(outs, axis=2)
    y = jnp.einsum('bsgc,gcd->bsgd', d, pool_w.astype(jnp.float32)).reshape(B, S, D_TOK)
    return (y * pool_scale.astype(jnp.float32)).astype(u.dtype)


def stick_breaking(q, k, v):
    B, S, H, dh = q.shape
    nb = S // Q_BLOCK
    scale = dh ** -0.5
    qt = q.transpose(0, 2, 1, 3)
    kt = k.transpose(0, 2, 1, 3)
    vt = v.transpose(0, 2, 1, 3)
    idx = jnp.arange(Q_BLOCK)
    later_mat = (idx[:, None] > idx[None, :]).astype(jnp.float32)
    outs = []
    for bi in range(nb):
        nk = bi + 1
        K = nk * Q_BLOCK
        qb = qt[:, :, bi * Q_BLOCK:(bi + 1) * Q_BLOCK]
        z = jnp.einsum('bhqd,bhkd->bhqk', qb, kt[:, :, :K]).astype(jnp.float32) * scale
        qpos = bi * Q_BLOCK + idx
        mask = jnp.arange(K)[None, :] < qpos[:, None]
        ls = jax.nn.log_sigmoid(z)
        lf = jnp.where(mask, ls - z, 0.0)
        lfb = lf.reshape(B, H, Q_BLOCK, nk, Q_BLOCK)
        within = jnp.einsum('bhqnj,js->bhqns', lfb, later_mat,
                            precision=lax.Precision.HIGHEST)
        bsum = jnp.sum(lfb, axis=-1)
        later = lax.cumsum(bsum, axis=3, reverse=True) - bsum
        logw = ls.reshape(B, H, Q_BLOCK, nk, Q_BLOCK) + within + later[..., None]
        a = jnp.where(mask, jnp.exp(logw).reshape(B, H, Q_BLOCK, K), 0.0)
        outs.append(jnp.einsum('bhqk,bhkd->bhqd', a.astype(vt.dtype), vt[:, :, :K]))
    o = jnp.concatenate(outs, axis=2)
    return o.transpose(0, 2, 1, 3).reshape(B, S, H * dh)


def mem_attention(qm, mem_n, w_kv):
    B, S, _ = qm.shape
    L = mem_n.shape[1]
    kv = mem_n @ w_kv
    km = kv[..., :D_MEMH].reshape(B, L, N_MEM_HEADS, HEAD_DIM)
    vm = kv[..., D_MEMH:].reshape(B, L, N_MEM_HEADS, HEAD_DIM)
    qh = qm.reshape(B, S, N_MEM_HEADS, HEAD_DIM)
    s = jnp.einsum('bqhd,bkhd->bhqk', qh, km).astype(jnp.float32) * (HEAD_DIM ** -0.5)
    p = jax.nn.softmax(s, axis=-1)
    o = jnp.einsum('bhqk,bkhd->bqhd', p.astype(vm.dtype), vm)
    return o.reshape(B, S, D_MEMH)


def setup_inputs(seed: int = 0) -> dict:
    key = jax.random.key(seed)
    ks = jax.random.split(key, 17)
    f32 = jnp.float32

    def w(k, shape, fan_in):
        return jax.random.normal(k, shape, f32) * (fan_in ** -0.5)

    return {
        "x": jax.random.normal(ks[0], (BATCH, SEQ, D_MODEL), f32),
        "mem": jax.random.normal(ks[1], (BATCH, MEM_LEN, D_MODEL), f32),
        "g_pre": 1.0 + 0.05 * jax.random.normal(ks[2], (DEPTH, 3, D_MODEL), f32),
        "g_post": 1.0 + 0.05 * jax.random.normal(ks[3], (DEPTH, 3, D_MODEL), f32),
        "g_mem": 1.0 + 0.05 * jax.random.normal(ks[4], (DEPTH, D_MODEL), f32),
        "ffn1_gate": w(ks[5], (DEPTH, D_MODEL, D_FF), D_MODEL),
        "ffn1_up": w(ks[6], (DEPTH, D_MODEL, D_FF), D_MODEL),
        "ffn1_down": w(ks[7], (DEPTH, D_FF, D_MODEL), D_FF),
        "ffn2_gate": w(ks[8], (DEPTH, D_MODEL, D_FF), D_MODEL),
        "ffn2_up": w(ks[9], (DEPTH, D_MODEL, D_FF), D_MODEL),
        "ffn2_down": w(ks[10], (DEPTH, D_FF, D_MODEL), D_FF),
        "w_in_pool": w(ks[11], (N_A, D_MODEL, D_TOK + D_MEMH), D_MODEL),
        "pool_w": w(ks[12], (N_A, N_POOL_GROUPS, POOL_GROUP, POOL_GROUP), POOL_GROUP),
        "pool_scale": 1.0 + 0.1 * jax.random.normal(ks[13], (N_A, D_TOK), f32),
        "w_in_sb": w(ks[14], (N_B, D_MODEL, 3 * D_TOK + D_MEMH), D_MODEL),
        "w_mem_kv": w(ks[15], (DEPTH, D_MODEL, 2 * D_MEMH), D_MODEL),
        "w_out": w(ks[16], (DEPTH, D_MIX, D_MODEL), D_MIX),
    }


def reference(x, mem, g_pre, g_post, g_mem, ffn1_gate, ffn1_up, ffn1_down,
              ffn2_gate, ffn2_up, ffn2_down, w_in_pool, pool_w, pool_scale,
              w_in_sb, w_mem_kv, w_out):
    B, S, _ = x.shape
    h = x
    for i in range(DEPTH):
        f = swiglu(rmsnorm(h, g_pre[i, 0]), ffn1_gate[i], ffn1_up[i], ffn1_down[i])
        h = h + 0.5 * rmsnorm(f, g_post[i, 0])

        u = rmsnorm(h, g_pre[i, 1])
        mem_n = rmsnorm(mem, g_mem[i])
        j = i // N_MIXERS
        if i % N_MIXERS == 0:
            proj = u @ w_in_pool[j]
            tok = pool_mixer(proj[..., :D_TOK], pool_w[j], pool_scale[j])
            qm = proj[..., D_TOK:]
        else:
            proj = u @ w_in_sb[j]
            q = proj[..., :D_TOK].reshape(B, S, N_SB_HEADS, HEAD_DIM)
            k = proj[..., D_TOK:2 * D_TOK].reshape(B, S, N_SB_HEADS, HEAD_DIM)
            v = proj[..., 2 * D_TOK:3 * D_TOK].reshape(B, S, N_SB_HEADS, HEAD_DIM)
            tok = stick_breaking(q, k, v)
            qm = proj[..., 3 * D_TOK:]
        mo = mem_attention(qm, mem_n, w_mem_kv[i])
        mix = jnp.concatenate([tok, mo], axis=-1) @ w_out[i]
        h = h + rmsnorm(mix, g_post[i, 1])

        f = swiglu(rmsnorm(h, g_pre[i, 2]), ffn2_gate[i], ffn2_up[i], ffn2_down[i])
        h = h + 0.5 * rmsnorm(f, g_post[i, 2])
    return h
```

```python
import functools

import jax
import jax.numpy as jnp
from jax import lax
from jax.experimental import pallas as pl
from jax.experimental.pallas import tpu as pltpu

F32 = jnp.float32
BF16 = jnp.bfloat16

EPS = 1e-6
HEAD_DIM = 64
HEAD_PAIR = 2 * HEAD_DIM
POOL_WINDOWS = (2, 4, 8, 16)
POOL_GROUP = 128
POOL_HALO = 16
D_TOK = 512
D_MEMH = 256
N_MEM_HEADS = D_MEMH // HEAD_DIM
SB_TILE = 256
FF_CHUNK = 512
TOKEN_TILE = 512
VMEM_LIMIT = 56 * 1024 * 1024


def _rms(x, g):
    ms = jnp.mean(x * x, axis=-1, keepdims=True)
    return x * lax.rsqrt(ms + EPS) * g


def _ffn_half_step(x, g_pre, wg_ref, wu_ref, wd_ref, g_post):
    xn = _rms(x, g_pre).astype(BF16)
    d_ff = wg_ref.shape[1]
    f = None
    for c in range(d_ff // FF_CHUNK):
        cs = slice(c * FF_CHUNK, (c + 1) * FF_CHUNK)
        gate = jnp.dot(xn, wg_ref[:, cs], preferred_element_type=F32)
        up = jnp.dot(xn, wu_ref[:, cs], preferred_element_type=F32)
        mid = (gate * (1.0 / (1.0 + jnp.exp(-gate))) * up).astype(BF16)
        part = jnp.dot(mid, wd_ref[cs, :], preferred_element_type=F32)
        f = part if f is None else f + part
    return x + 0.5 * _rms(f, g_post)


def _mem_attention(qm, kmt_ref, vm_ref):
    t = qm.shape[0]
    lane = lax.broadcasted_iota(jnp.int32, (t, D_MEMH), 1)
    kmt = kmt_ref[...]
    vm = vm_ref[...]
    out = jnp.zeros((t, D_MEMH), F32)
    for h in range(N_MEM_HEADS):
        sel = (lane >= h * HEAD_DIM) & (lane < (h + 1) * HEAD_DIM)
        qh = jnp.where(sel, qm, jnp.zeros_like(qm))
        s = jnp.dot(qh, kmt, preferred_element_type=F32) * (HEAD_DIM ** -0.5)
        e = jnp.exp(s - jnp.max(s, axis=-1, keepdims=True))
        p = (e / jnp.sum(e, axis=-1, keepdims=True)).astype(BF16)
        o = jnp.dot(p, vm, preferred_element_type=F32)
        out = jnp.where(sel, o, out)
    return out.astype(BF16)


def _memkv_kernel(mem_ref, g_ref, wkt_ref, wv_ref, kmt_ref, vm_ref):
    mn = _rms(mem_ref[...], g_ref[...]).astype(BF16)
    kmt = lax.dot_general(wkt_ref[...], mn, (((1,), (1,)), ((), ())),
                          preferred_element_type=F32)
    kmt_ref[...] = kmt.astype(BF16)
    vm_ref[...] = jnp.dot(mn, wv_ref[...], preferred_element_type=F32).astype(BF16)


def _memkv_call(mem, g_mem, w_kt, w_v):
    b, l, d = mem.shape
    depth = g_mem.shape[0]
    return pl.pallas_call(
        _memkv_kernel,
        grid=(b, depth),
        in_specs=[
            pl.BlockSpec((None, l, d), lambda bi, i: (bi, 0, 0)),
            pl.BlockSpec((None, 1, d), lambda bi, i: (i, 0, 0)),
            pl.BlockSpec((None, D_MEMH, d), lambda bi, i: (i, 0, 0)),
            pl.BlockSpec((None, d, D_MEMH), lambda bi, i: (i, 0, 0)),
        ],
        out_specs=[
            pl.BlockSpec((None, None, D_MEMH, l), lambda bi, i: (i, bi, 0, 0)),
            pl.BlockSpec((None, None, l, D_MEMH), lambda bi, i: (i, bi, 0, 0)),
        ],
        out_shape=[
            jax.ShapeDtypeStruct((depth, b, D_MEMH, l), BF16),
            jax.ShapeDtypeStruct((depth, b, l, D_MEMH), BF16),
        ],
        compiler_params=pltpu.CompilerParams(
            dimension_semantics=("arbitrary", "arbitrary"), vmem_limit_bytes=VMEM_LIMIT),
        name="memkv",
    )(mem, g_mem.reshape(depth, 1, d), w_kt, w_v)


def _const_spec(shape, layer):
    nd = len(shape)
    return pl.BlockSpec((None,) + tuple(shape), lambda bi, j: (layer,) + (0,) * nd,
                        pipeline_mode=pl.Buffered(1))


def _ffn_specs(d, d_ff, layer, which):
    g_idx = 3 * layer + (0 if which == 1 else 2)
    return [
        _const_spec((1, d), g_idx),
        _const_spec((d, d_ff), layer),
        _const_spec((d, d_ff), layer),
        _const_spec((d_ff, d), layer),
        _const_spec((1, d), g_idx),
    ]


def _head_kernel(*refs, mixer, n_ffn):
    it = iter(refs)
    h_ref = next(it)
    ffn_refs = [tuple(next(it) for _ in range(5)) for _ in range(n_ffn)]
    g_mix_ref = next(it)
    x = h_ref[...]
    if mixer == "pool":
        w_in_ref = next(it)
        h_out_ref, tok_ref, qm_ref = it
    else:
        w_q_ref, w_kt_ref, w_v_ref, w_qm_ref = (next(it) for _ in range(4))
        h_out_ref, q_ref, kt_ref, v_ref, qm_ref = it
    for g_pre, wg, wu, wd, g_post in ffn_refs:
        x = _ffn_half_step(x, g_pre[...], wg, wu, wd, g_post[...])
    h_out_ref[...] = x
    u = _rms(x, g_mix_ref[...]).astype(BF16)
    if mixer == "pool":
        proj = jnp.dot(u, w_in_ref[...], preferred_element_type=F32)
        tok_ref[...] = proj[:, :D_TOK]
        qm_ref[...] = proj[:, D_TOK:].astype(BF16)
    else:
        q = jnp.dot(u, w_q_ref[...], preferred_element_type=F32)
        q_ref[...] = (q * (HEAD_DIM ** -0.5)).astype(BF16)
        kt = lax.dot_general(w_kt_ref[...], u, (((1,), (1,)), ((), ())),
                             preferred_element_type=F32)
        for p in range(D_TOK // HEAD_PAIR):
            for t in range(kt.shape[1] // SB_TILE):
                kt_ref[p, t] = kt[p * HEAD_PAIR:(p + 1) * HEAD_PAIR,
                                  t * SB_TILE:(t + 1) * SB_TILE].astype(BF16)
        v_ref[...] = jnp.dot(u, w_v_ref[...], preferred_element_type=F32).astype(BF16)
        qm_ref[...] = jnp.dot(u, w_qm_ref[...], preferred_element_type=F32).astype(BF16)


def _head_call(h, params, layer, mixer, ffn_list):
    b, s, d = h.shape
    tm = min(TOKEN_TILE, s)
    d_ff = params["ffn1_gate"].shape[2]
    tile = lambda w: pl.BlockSpec((None, tm, w), lambda bi, j: (bi, j, 0))
    in_specs = [tile(d)]
    args = [h]
    for (li, which) in ffn_list:
        in_specs += _ffn_specs(d, d_ff, li, which)
        pre = "ffn1" if which == 1 else "ffn2"
        args += [params["g_pre"], params[pre + "_gate"], params[pre + "_up"],
                 params[pre + "_down"], params["g_post"]]
    in_specs.append(_const_spec((1, d), 3 * layer + 1))
    args.append(params["g_pre"])
    j = layer // 2
    if mixer == "pool":
        in_specs.append(_const_spec((d, D_TOK + D_MEMH), j))
        args.append(params["w_in_pool"])
        out_specs = [tile(d), tile(D_TOK), tile(D_MEMH)]
        out_shape = [jax.ShapeDtypeStruct((b, s, d), F32),
                     jax.ShapeDtypeStruct((b, s, D_TOK), F32),
                     jax.ShapeDtypeStruct((b, s, D_MEMH), BF16)]
    else:
        in_specs += [_const_spec((d, D_TOK), j), _const_spec((D_TOK, d), j),
                     _const_spec((d, D_TOK), j), _const_spec((d, D_MEMH), j)]
        args += [params["w_sb_q"], params["w_sb_kt"], params["w_sb_v"], params["w_sb_qm"]]
        n_pair = D_TOK // HEAD_PAIR
        out_specs = [tile(d), tile(D_TOK),
                     pl.BlockSpec((None, n_pair, tm // SB_TILE, HEAD_PAIR, SB_TILE),
                                  lambda bi, j: (bi, 0, j, 0, 0)),
                     tile(D_TOK), tile(D_MEMH)]
        out_shape = [jax.ShapeDtypeStruct((b, s, d), F32),
                     jax.ShapeDtypeStruct((b, s, D_TOK), BF16),
                     jax.ShapeDtypeStruct((b, n_pair, s // SB_TILE, HEAD_PAIR, SB_TILE), BF16),
                     jax.ShapeDtypeStruct((b, s, D_TOK), BF16),
                     jax.ShapeDtypeStruct((b, s, D_MEMH), BF16)]
    return pl.pallas_call(
        functools.partial(_head_kernel, mixer=mixer, n_ffn=len(ffn_list)),
        grid=(b, s // tm),
        in_specs=in_specs, out_specs=out_specs, out_shape=out_shape,
        compiler_params=pltpu.CompilerParams(
            dimension_semantics=("arbitrary", "arbitrary"), vmem_limit_bytes=VMEM_LIMIT),
        name="head_" + mixer,
    )(*args)


def _pool_mixer(u, ext_ref, pw_ref, ps, first_tile, tile_start):
    t = u.shape[0]

    @pl.when(first_tile)
    def _():
        ext_ref[0:POOL_HALO, :] = jnp.zeros((POOL_HALO, D_TOK), F32)

    @pl.when(jnp.logical_not(first_tile))
    def _():
        ext_ref[0:POOL_HALO, :] = ext_ref[t:t + POOL_HALO, :]

    ext_ref[POOL_HALO:POOL_HALO + t, :] = u
    pos = tile_start + lax.broadcasted_iota(jnp.int32, (t, POOL_GROUP), 0)
    outs = []
    for gi, w in enumerate(POOL_WINDOWS):
        cs = slice(gi * POOL_GROUP, (gi + 1) * POOL_GROUP)
        win = ext_ref[POOL_HALO:POOL_HALO + t, cs]
        for back in range(1, w):
            win = win + ext_ref[POOL_HALO - back:POOL_HALO - back + t, cs]
        cnt = jnp.minimum(pos + 1, w).astype(F32)
        dlt = (win / cnt - u[:, cs]).astype(BF16)
        outs.append(jnp.dot(dlt, pw_ref[gi], preferred_element_type=F32))
    return (jnp.concatenate(outs, axis=-1) * ps).astype(BF16)


def _tail_kernel(*refs, mixer, n_ffn):
    it = iter(refs)
    h_ref, tok_ref, qm_ref, kmt_ref, vm_ref = (next(it) for _ in range(5))
    if mixer == "pool":
        pw_ref, ps_ref = next(it), next(it)
    w_out_ref, g_post_ref = next(it), next(it)
    ffn_refs = [tuple(next(it) for _ in range(5)) for _ in range(n_ffn)]
    h_out_ref = next(it)
    x = h_ref[...]
    if mixer == "pool":
        ext_ref = next(it)
        j = pl.program_id(1)
        tok = _pool_mixer(tok_ref[...], ext_ref, pw_ref, ps_ref[...], j == 0,
                          j * tok_ref.shape[0])
    else:
        tok = tok_ref[...]
    mo = _mem_attention(qm_ref[...], kmt_ref, vm_ref)
    cat = jnp.concatenate([tok, mo], axis=-1)
    mix = jnp.dot(cat, w_out_ref[...], preferred_element_type=F32)
    x = x + _rms(mix, g_post_ref[...])
    for g_pre, wg, wu, wd, g_post in ffn_refs:
        x = _ffn_half_step(x, g_pre[...], wg, wu, wd, g_post[...])
    h_out_ref[...] = x


def _tail_call(h, tok, qm, kmt, vm, params, layer, mixer, ffn_list):
    b, s, d = h.shape
    tm = min(TOKEN_TILE, s)
    d_ff = params["ffn1_gate"].shape[2]
    l = kmt.shape[-1]
    tile = lambda w: pl.BlockSpec((None, tm, w), lambda bi, j: (bi, j, 0))
    in_specs = [tile(d), tile(D_TOK), tile(D_MEMH),
                pl.BlockSpec((None, None, D_MEMH, l), lambda bi, j: (layer, bi, 0, 0)),
                pl.BlockSpec((None, None, l, D_MEMH), lambda bi, j: (layer, bi, 0, 0))]
    args = [h, tok, qm, kmt, vm]
    scratch = []
    if mixer == "pool":
        jl = layer // 2
        in_specs += [_const_spec((len(POOL_WINDOWS), POOL_GROUP, POOL_GROUP), jl),
                     _const_spec((1, D_TOK), jl)]
        args += [params["pool_w"], params["pool_scale"]]
        scratch = [pltpu.VMEM((POOL_HALO + tm, D_TOK), F32)]
    in_specs += [_const_spec((D_TOK + D_MEMH, d), layer), _const_spec((1, d), 3 * layer + 1)]
    args += [params["w_out"], params["g_post"]]
    for (li, which) in ffn_list:
        in_specs += _ffn_specs(d, d_ff, li, which)
        pre = "ffn1" if which == 1 else "ffn2"
        args += [params["g_pre"], params[pre + "_gate"], params[pre + "_up"],
                 params[pre + "_down"], params["g_post"]]
    return pl.pallas_call(
        functools.partial(_tail_kernel, mixer=mixer, n_ffn=len(ffn_list)),
        grid=(b, s // tm),
        in_specs=in_specs, out_specs=tile(d),
        out_shape=jax.ShapeDtypeStruct((b, s, d), F32),
        scratch_shapes=scratch,
        compiler_params=pltpu.CompilerParams(
            dimension_semantics=("arbitrary", "arbitrary"), vmem_limit_bytes=VMEM_LIMIT),
        name="tail_" + mixer,
    )(*args)


def _sb_tile(z, tri, v_tile, later, mask):
    sp = jnp.log(1.0 + jnp.exp(-jnp.abs(z)))
    ls = jnp.minimum(z, 0.0) - sp
    lf = ls - z
    if mask is not None:
        lf = jnp.where(mask, lf, 0.0)
    hi = lf.astype(BF16)
    lo = (lf - hi.astype(F32)).astype(BF16)
    within = (jnp.dot(hi, tri, preferred_element_type=F32)
              + jnp.dot(lo, tri, preferred_element_type=F32))
    a = jnp.exp(ls + within + later)
    if mask is not None:
        a = jnp.where(mask, a, 0.0)
    pv = jnp.dot(a.astype(BF16), v_tile, preferred_element_type=F32)
    return pv, later + jnp.sum(lf, axis=-1, keepdims=True)


def _sb_kernel(q_ref, kt_ref, v_ref, o_ref, acc_ref):
    qi = pl.program_id(2)
    tq = q_ref.shape[0]
    tk = kt_ref.shape[2]
    q2 = q_ref[...]
    lane = lax.broadcasted_iota(jnp.int32, (tq, HEAD_PAIR), 1)
    first = lane < HEAD_DIM
    zero = jnp.zeros_like(q2)
    q_heads = (jnp.where(first, q2, zero), jnp.where(first, zero, q2))
    row = lax.broadcasted_iota(jnp.int32, (tq, tk), 0)
    col = lax.broadcasted_iota(jnp.int32, (tq, tk), 1)
    tri = jnp.where(row > col, 1.0, 0.0).astype(BF16)
    mask = col < row

    def step(kj, laters, mask):
        kt = kt_ref[kj]
        vt = v_ref[pl.ds(pl.multiple_of(kj * tk, tk), tk), :]
        new = []
        for hd in range(2):
            z = jnp.dot(q_heads[hd], kt, preferred_element_type=F32)
            pv, later = _sb_tile(z, tri, vt, laters[hd], mask)
            if mask is not None:
                acc_ref[hd] = pv
            else:
                acc_ref[hd] += pv
            new.append(later)
        return tuple(new)

    zeros = jnp.zeros((tq, 1), F32)
    laters = step(qi, (zeros, zeros), mask)
    lax.fori_loop(0, qi, lambda i, c: step(qi - 1 - i, c, None), laters)
    o_ref[...] = jnp.where(first, acc_ref[0], acc_ref[1]).astype(BF16)


def _sb_call(q, kt, v):
    b, s, _ = q.shape
    n_pair = D_TOK // HEAD_PAIR
    nk = s // SB_TILE
    return pl.pallas_call(
        _sb_kernel,
        grid=(b, n_pair, nk),
        in_specs=[
            pl.BlockSpec((None, SB_TILE, HEAD_PAIR), lambda bi, p, qi: (bi, qi, p)),
            pl.BlockSpec((None, None, nk, HEAD_PAIR, SB_TILE), lambda bi, p, qi: (bi, p, 0, 0, 0)),
            pl.BlockSpec((None, s, HEAD_PAIR), lambda bi, p, qi: (bi, 0, p)),
        ],
        out_specs=pl.BlockSpec((None, SB_TILE, HEAD_PAIR), lambda bi, p, qi: (bi, qi, p)),
        out_shape=jax.ShapeDtypeStruct((b, s, D_TOK), BF16),
        scratch_shapes=[pltpu.VMEM((2, SB_TILE, HEAD_PAIR), F32)],
        compiler_params=pltpu.CompilerParams(
            dimension_semantics=("arbitrary", "arbitrary", "arbitrary"),
            vmem_limit_bytes=VMEM_LIMIT),
        name="stick_breaking",
    )(q, kt, v)


def kernel(x, mem, g_pre, g_post, g_mem, ffn1_gate, ffn1_up, ffn1_down, ffn2_gate, ffn2_up,
           ffn2_down, w_in_pool, pool_w, pool_scale, w_in_sb, w_mem_kv, w_out):
    depth, _, d = g_pre.shape
    assert x.shape[1] % max(SB_TILE, TOKEN_TILE) == 0 or x.shape[1] == SB_TILE
    params = {
        "g_pre": g_pre.reshape(depth * 3, 1, d),
        "g_post": g_post.reshape(depth * 3, 1, d),
        "ffn1_gate": ffn1_gate.astype(BF16), "ffn1_up": ffn1_up.astype(BF16),
        "ffn1_down": ffn1_down.astype(BF16),
        "ffn2_gate": ffn2_gate.astype(BF16), "ffn2_up": ffn2_up.astype(BF16),
        "ffn2_down": ffn2_down.astype(BF16),
        "w_in_pool": w_in_pool.astype(BF16),
        "pool_w": pool_w.astype(BF16),
        "pool_scale": pool_scale.reshape(pool_scale.shape[0], 1, D_TOK),
        "w_sb_q": w_in_sb[:, :, :D_TOK].astype(BF16),
        "w_sb_kt": jnp.swapaxes(w_in_sb[:, :, D_TOK:2 * D_TOK], 1, 2).astype(BF16),
        "w_sb_v": w_in_sb[:, :, 2 * D_TOK:3 * D_TOK].astype(BF16),
        "w_sb_qm": w_in_sb[:, :, 3 * D_TOK:].astype(BF16),
        "w_out": w_out.astype(BF16),
    }
    kmt, vm = _memkv_call(mem, g_mem,
                          jnp.swapaxes(w_mem_kv[:, :, :D_MEMH], 1, 2).astype(BF16),
                          w_mem_kv[:, :, D_MEMH:].astype(BF16))
    h = x
    for i in range(depth):
        mixer = "pool" if i % 2 == 0 else "sb"
        if mixer == "pool":
            h, tok, qm = _head_call(h, params, i, mixer, [(i, 1)])
        else:
            h, q, kt, v, qm = _head_call(h, params, i, mixer, [(i, 1)])
            tok = _sb_call(q, kt, v)
        h = _tail_call(h, tok, qm, kmt, vm, params, i, mixer, [(i, 2)])
    return h
```

```python
import functools

import jax
import jax.numpy as jnp
from jax import lax
from jax.experimental import pallas as pl
from jax.experimental.pallas import tpu as pltpu

F32 = jnp.float32
BF16 = jnp.bfloat16

EPS = 1e-6
HEAD_DIM = 64
SB_HEADS = 4
SB_LANES = SB_HEADS * HEAD_DIM
MASK_BIAS = -1e30
SB_ROWS = 32
LOG2E = 1.4426950408889634
POOL_WINDOWS = (2, 4, 8, 16)
POOL_GROUP = 128
POOL_HALO = 16
D_TOK = 512
D_MEMH = 256
N_MEM_HEADS = D_MEMH // HEAD_DIM
SB_TILE = 256
FF_CHUNK = 512
TOKEN_TILE = 512
VMEM_LIMIT = 56 * 1024 * 1024


def _rms(x, g):
    ms = jnp.mean(x * x, axis=-1, keepdims=True)
    return x * lax.rsqrt(ms + EPS) * g


def _ffn_half_step(x, g_pre, wg_ref, wu_ref, wd_ref, g_post):
    xn = _rms(x, g_pre).astype(BF16)
    d_ff = wg_ref.shape[1]
    f = None
    for c in range(d_ff // FF_CHUNK):
        cs = slice(c * FF_CHUNK, (c + 1) * FF_CHUNK)
        gate = jnp.dot(xn, wg_ref[:, cs], preferred_element_type=F32)
        up = jnp.dot(xn, wu_ref[:, cs], preferred_element_type=F32)
        mid = (gate * (1.0 / (1.0 + jnp.exp(-gate))) * up).astype(BF16)
        part = jnp.dot(mid, wd_ref[cs, :], preferred_element_type=F32)
        f = part if f is None else f + part
    return x + 0.5 * _rms(f, g_post)


def _mem_attention(qm, kmt_ref, vm_ref):
    t = qm.shape[0]
    lane = lax.broadcasted_iota(jnp.int32, (t, D_MEMH), 1)
    kmt = kmt_ref[...]
    vm = vm_ref[...]
    out = jnp.zeros((t, D_MEMH), F32)
    for h in range(N_MEM_HEADS):
        sel = (lane >= h * HEAD_DIM) & (lane < (h + 1) * HEAD_DIM)
        qh = jnp.where(sel, qm, jnp.zeros_like(qm))
        s = jnp.dot(qh, kmt, preferred_element_type=F32) * (HEAD_DIM ** -0.5)
        e = jnp.exp(s - jnp.max(s, axis=-1, keepdims=True))
        p = (e / jnp.sum(e, axis=-1, keepdims=True)).astype(BF16)
        o = jnp.dot(p, vm, preferred_element_type=F32)
        out = jnp.where(sel, o, out)
    return out.astype(BF16)


def _memkv_kernel(mem_ref, g_ref, wkt_ref, wv_ref, kmt_ref, vm_ref):
    mn = _rms(mem_ref[...], g_ref[...]).astype(BF16)
    kmt = lax.dot_general(wkt_ref[...], mn, (((1,), (1,)), ((), ())),
                          preferred_element_type=F32)
    kmt_ref[...] = kmt.astype(BF16)
    vm_ref[...] = jnp.dot(mn, wv_ref[...], preferred_element_type=F32).astype(BF16)


def _memkv_call(mem, g_mem, w_kt, w_v):
    b, l, d = mem.shape
    depth = g_mem.shape[0]
    return pl.pallas_call(
        _memkv_kernel,
        grid=(b, depth),
        in_specs=[
            pl.BlockSpec((None, l, d), lambda bi, i: (bi, 0, 0)),
            pl.BlockSpec((None, 1, d), lambda bi, i: (i, 0, 0)),
            pl.BlockSpec((None, D_MEMH, d), lambda bi, i: (i, 0, 0)),
            pl.BlockSpec((None, d, D_MEMH), lambda bi, i: (i, 0, 0)),
        ],
        out_specs=[
            pl.BlockSpec((None, None, D_MEMH, l), lambda bi, i: (i, bi, 0, 0)),
            pl.BlockSpec((None, None, l, D_MEMH), lambda bi, i: (i, bi, 0, 0)),
        ],
        out_shape=[
            jax.ShapeDtypeStruct((depth, b, D_MEMH, l), BF16),
            jax.ShapeDtypeStruct((depth, b, l, D_MEMH), BF16),
        ],
        compiler_params=pltpu.CompilerParams(
            dimension_semantics=("arbitrary", "arbitrary"), vmem_limit_bytes=VMEM_LIMIT),
        name="memkv",
    )(mem, g_mem.reshape(depth, 1, d), w_kt, w_v)


def _const_spec(shape, layer):
    nd = len(shape)
    return pl.BlockSpec((None,) + tuple(shape), lambda bi, j: (layer,) + (0,) * nd,
                        pipeline_mode=pl.Buffered(1))


def _ffn_specs(d, d_ff, layer, which):
    g_idx = 3 * layer + (0 if which == 1 else 2)
    return [
        _const_spec((1, d), g_idx),
        _const_spec((d, d_ff), layer),
        _const_spec((d, d_ff), layer),
        _const_spec((d_ff, d), layer),
        _const_spec((1, d), g_idx),
    ]


def _head_kernel(*refs, mixer, n_ffn):
    it = iter(refs)
    h_ref = next(it)
    ffn_refs = [tuple(next(it) for _ in range(5)) for _ in range(n_ffn)]
    g_mix_ref = next(it)
    x = h_ref[...]
    if mixer == "pool":
        w_in_ref = next(it)
        h_out_ref, tok_ref, qm_ref = it
    else:
        w_q_ref, w_kt_ref, w_v_ref, w_qm_ref = (next(it) for _ in range(4))
        h_out_ref, q_ref, kt_ref, v_ref, qm_ref = it
    for g_pre, wg, wu, wd, g_post in ffn_refs:
        x = _ffn_half_step(x, g_pre[...], wg, wu, wd, g_post[...])
    h_out_ref[...] = x
    u = _rms(x, g_mix_ref[...]).astype(BF16)
    if mixer == "pool":
        proj = jnp.dot(u, w_in_ref[...], preferred_element_type=F32)
        tok_ref[...] = proj[:, :D_TOK]
        qm_ref[...] = proj[:, D_TOK:].astype(BF16)
    else:
        q = jnp.dot(u, w_q_ref[...], preferred_element_type=F32)
        q_ref[...] = (q * (HEAD_DIM ** -0.5)).astype(BF16)
        kt = lax.dot_general(w_kt_ref[...], u, (((1,), (1,)), ((), ())),
                             preferred_element_type=F32)
        for g in range(D_TOK // SB_LANES):
            for t in range(kt.shape[1] // SB_TILE):
                kt_ref[g, t] = kt[g * SB_LANES:(g + 1) * SB_LANES,
                                  t * SB_TILE:(t + 1) * SB_TILE].astype(BF16)
        v_ref[...] = jnp.dot(u, w_v_ref[...], preferred_element_type=F32).astype(BF16)
        qm_ref[...] = jnp.dot(u, w_qm_ref[...], preferred_element_type=F32).astype(BF16)


def _head_call(h, params, layer, mixer, ffn_list):
    b, s, d = h.shape
    tm = min(TOKEN_TILE, s)
    d_ff = params["ffn1_gate"].shape[2]
    tile = lambda w: pl.BlockSpec((None, tm, w), lambda bi, j: (bi, j, 0))
    in_specs = [tile(d)]
    args = [h]
    for (li, which) in ffn_list:
        in_specs += _ffn_specs(d, d_ff, li, which)
        pre = "ffn1" if which == 1 else "ffn2"
        args += [params["g_pre"], params[pre + "_gate"], params[pre + "_up"],
                 params[pre + "_down"], params["g_post"]]
    in_specs.append(_const_spec((1, d), 3 * layer + 1))
    args.append(params["g_pre"])
    j = layer // 2
    if mixer == "pool":
        in_specs.append(_const_spec((d, D_TOK + D_MEMH), j))
        args.append(params["w_in_pool"])
        out_specs = [tile(d), tile(D_TOK), tile(D_MEMH)]
        out_shape = [jax.ShapeDtypeStruct((b, s, d), F32),
                     jax.ShapeDtypeStruct((b, s, D_TOK), F32),
                     jax.ShapeDtypeStruct((b, s, D_MEMH), BF16)]
    else:
        in_specs += [_const_spec((d, D_TOK), j), _const_spec((D_TOK, d), j),
                     _const_spec((d, D_TOK), j), _const_spec((d, D_MEMH), j)]
        args += [params["w_sb_q"], params["w_sb_kt"], params["w_sb_v"], params["w_sb_qm"]]
        n_group = D_TOK // SB_LANES
        out_specs = [tile(d), tile(D_TOK),
                     pl.BlockSpec((None, n_group, tm // SB_TILE, SB_LANES, SB_TILE),
                                  lambda bi, j: (bi, 0, j, 0, 0)),
                     tile(D_TOK), tile(D_MEMH)]
        out_shape = [jax.ShapeDtypeStruct((b, s, d), F32),
                     jax.ShapeDtypeStruct((b, s, D_TOK), BF16),
                     jax.ShapeDtypeStruct((b, n_group, s // SB_TILE, SB_LANES, SB_TILE), BF16),
                     jax.ShapeDtypeStruct((b, s, D_TOK), BF16),
                     jax.ShapeDtypeStruct((b, s, D_MEMH), BF16)]
    return pl.pallas_call(
        functools.partial(_head_kernel, mixer=mixer, n_ffn=len(ffn_list)),
        grid=(b, s // tm),
        in_specs=in_specs, out_specs=out_specs, out_shape=out_shape,
        compiler_params=pltpu.CompilerParams(
            dimension_semantics=("arbitrary", "arbitrary"), vmem_limit_bytes=VMEM_LIMIT),
        name="head_" + mixer,
    )(*args)


def _pool_mixer(u, ext_ref, pw_ref, ps, first_tile, tile_start):
    t = u.shape[0]

    @pl.when(first_tile)
    def _():
        ext_ref[0:POOL_HALO, :] = jnp.zeros((POOL_HALO, D_TOK), F32)

    @pl.when(jnp.logical_not(first_tile))
    def _():
        ext_ref[0:POOL_HALO, :] = ext_ref[t:t + POOL_HALO, :]

    ext_ref[POOL_HALO:POOL_HALO + t, :] = u
    pos = tile_start + lax.broadcasted_iota(jnp.int32, (t, POOL_GROUP), 0)
    outs = []
    for gi, w in enumerate(POOL_WINDOWS):
        cs = slice(gi * POOL_GROUP, (gi + 1) * POOL_GROUP)
        win = ext_ref[POOL_HALO:POOL_HALO + t, cs]
        for back in range(1, w):
            win = win + ext_ref[POOL_HALO - back:POOL_HALO - back + t, cs]
        cnt = jnp.minimum(pos + 1, w).astype(F32)
        dlt = (win / cnt - u[:, cs]).astype(BF16)
        outs.append(jnp.dot(dlt, pw_ref[gi], preferred_element_type=F32))
    return (jnp.concatenate(outs, axis=-1) * ps).astype(BF16)


def _tail_kernel(*refs, mixer, n_ffn):
    it = iter(refs)
    h_ref, tok_ref, qm_ref, kmt_ref, vm_ref = (next(it) for _ in range(5))
    if mixer == "pool":
        pw_ref, ps_ref = next(it), next(it)
    w_out_ref, g_post_ref = next(it), next(it)
    ffn_refs = [tuple(next(it) for _ in range(5)) for _ in range(n_ffn)]
    h_out_ref = next(it)
    x = h_ref[...]
    if mixer == "pool":
        ext_ref = next(it)
        j = pl.program_id(1)
        tok = _pool_mixer(tok_ref[...], ext_ref, pw_ref, ps_ref[...], j == 0,
                          j * tok_ref.shape[0])
    else:
        tok = tok_ref[...]
    mo = _mem_attention(qm_ref[...], kmt_ref, vm_ref)
    cat = jnp.concatenate([tok, mo], axis=-1)
    mix = jnp.dot(cat, w_out_ref[...], preferred_element_type=F32)
    x = x + _rms(mix, g_post_ref[...])
    for g_pre, wg, wu, wd, g_post in ffn_refs:
        x = _ffn_half_step(x, g_pre[...], wg, wu, wd, g_post[...])
    h_out_ref[...] = x


def _tail_call(h, tok, qm, kmt, vm, params, layer, mixer, ffn_list):
    b, s, d = h.shape
    tm = min(TOKEN_TILE, s)
    d_ff = params["ffn1_gate"].shape[2]
    l = kmt.shape[-1]
    tile = lambda w: pl.BlockSpec((None, tm, w), lambda bi, j: (bi, j, 0))
    in_specs = [tile(d), tile(D_TOK), tile(D_MEMH),
                pl.BlockSpec((None, None, D_MEMH, l), lambda bi, j: (layer, bi, 0, 0)),
                pl.BlockSpec((None, None, l, D_MEMH), lambda bi, j: (layer, bi, 0, 0))]
    args = [h, tok, qm, kmt, vm]
    scratch = []
    if mixer == "pool":
        jl = layer // 2
        in_specs += [_const_spec((len(POOL_WINDOWS), POOL_GROUP, POOL_GROUP), jl),
                     _const_spec((1, D_TOK), jl)]
        args += [params["pool_w"], params["pool_scale"]]
        scratch = [pltpu.VMEM((POOL_HALO + tm, D_TOK), F32)]
    in_specs += [_const_spec((D_TOK + D_MEMH, d), layer), _const_spec((1, d), 3 * layer + 1)]
    args += [params["w_out"], params["g_post"]]
    for (li, which) in ffn_list:
        in_specs += _ffn_specs(d, d_ff, li, which)
        pre = "ffn1" if which == 1 else "ffn2"
        args += [params["g_pre"], params[pre + "_gate"], params[pre + "_up"],
                 params[pre + "_down"], params["g_post"]]
    return pl.pallas_call(
        functools.partial(_tail_kernel, mixer=mixer, n_ffn=len(ffn_list)),
        grid=(b, s // tm),
        in_specs=in_specs, out_specs=tile(d),
        out_shape=jax.ShapeDtypeStruct((b, s, d), F32),
        scratch_shapes=scratch,
        compiler_params=pltpu.CompilerParams(
            dimension_semantics=("arbitrary", "arbitrary"), vmem_limit_bytes=VMEM_LIMIT),
        name="tail_" + mixer,
    )(*args)


def _next_tile(qi, kj):
    wrap = kj == 0
    qn = qi + wrap.astype(jnp.int32)
    return qn, jnp.where(wrap, qn, kj - 1)


def _sb_kernel(q_ref, kt_ref, v_ref, o_ref, *scratch):
    z_refs, ls_refs, hl_refs, a_refs = (scratch[2 * i:2 * i + 2] for i in range(4))
    acc_ref, later_ref, vexp_ref = scratch[8:]
    nq = kt_ref.shape[0]
    tq = tk = SB_TILE
    n_tiles = nq * (nq + 1) // 2
    n_stage = 4
    heads = range(SB_HEADS)

    row = lax.broadcasted_iota(jnp.int32, (tq, tk), 0)
    col = lax.broadcasted_iota(jnp.int32, (tq, tk), 1)
    ntri = jnp.where(row >= col, -1.0, 0.0).astype(BF16)
    ntri2 = jnp.concatenate([ntri, ntri], axis=0)
    for ref in scratch[:10]:
        ref[...] = jnp.zeros(ref.shape, ref.dtype)

    lane = lax.broadcasted_iota(jnp.int32, (tk, SB_LANES), 1)

    def expand_values(t, _):
        vt = v_ref[pl.ds(pl.multiple_of(t * tk, tk), tk), :]
        for h in heads:
            own = (lane >= h * HEAD_DIM) & (lane < (h + 1) * HEAD_DIM)
            vexp_ref[t, h * tk:(h + 1) * tk, :] = jnp.where(own, vt, jnp.zeros_like(vt))
        return 0

    lax.fori_loop(0, nq, expand_values, 0)

    def step(cur, tiles):
        (q0, k0), _, _, (q3, k3) = tiles
        prev = 1 - cur
        z_ref, z_in = z_refs[cur], z_refs[prev]
        ls_ref, ls_in = ls_refs[cur], ls_refs[prev]
        hl_ref, hl_in = hl_refs[cur], hl_refs[prev]
        a_ref, a_in = a_refs[cur], a_refs[prev]

        pv = jnp.dot(a_in[...], vexp_ref[k3], preferred_element_type=F32)
        acc = jnp.where(q3 == k3, pv, acc_ref[...] + pv)
        acc_ref[...] = acc
        o_ref[pl.ds(pl.multiple_of(q3 * tq, tq), tq), :] = acc.astype(BF16)

        qc = jnp.minimum(q0, nq - 1)
        kc = jnp.minimum(k0, nq - 1)
        qt = q_ref[pl.ds(pl.multiple_of(qc * tq, tq), tq), :]
        kt = kt_ref[kc]
        for h in heads:
            kh = jnp.concatenate(
                [kt[g * HEAD_DIM:(g + 1) * HEAD_DIM] if g == h
                 else jnp.zeros((HEAD_DIM, tk), BF16) for g in heads], axis=0)
            z_ref[h] = jnp.dot(qt, kh, preferred_element_type=F32) * LOG2E

            rows = slice(h * tq, (h + 1) * tq)
            logw = ls_in[rows] + jnp.dot(hl_in[rows], ntri2, preferred_element_type=F32)
            a_ref[:, h * tk:(h + 1) * tk] = jnp.exp2(logw).astype(BF16)

            for r in range(0, tq, SB_ROWS):
                rows = slice(h * tq + r, h * tq + r + SB_ROWS)
                z = z_in[h, r:r + SB_ROWS]
                neg_abs = lax.bitcast_convert_type(
                    lax.bitcast_convert_type(z, jnp.int32) | jnp.int32(-2 ** 31), F32)
                nlf = jnp.maximum(z, 0.0) + jnp.log2(1.0 + jnp.exp2(neg_abs))
                later = later_ref[rows]
                ls_ref[rows] = z + later
                hi = nlf.astype(BF16)
                hl_ref[rows, 0:tk] = hi
                hl_ref[rows, tk:2 * tk] = (nlf - hi.astype(F32)).astype(BF16)
                later_ref[rows] = later - jnp.sum(nlf, axis=-1, keepdims=True)

        @pl.when(qc == kc)
        def _():
            bias = jnp.where(col < row, 0.0, MASK_BIAS)
            for h in heads:
                z_ref[h] += bias
            later_ref[...] = jnp.zeros(later_ref.shape, F32)

        return (_next_tile(q0, k0),) + tiles[:-1]

    zero = jnp.int32(0)
    n_iter = n_tiles + n_stage - 1
    tiles = lax.fori_loop(0, n_iter // 2, lambda _, c: step(1, step(0, c)),
                          ((zero, zero),) * n_stage)
    if n_iter % 2:
        step(0, tiles)


def _sb_call(q, kt, v):
    b, s, _ = q.shape
    n_group = D_TOK // SB_LANES
    nk = s // SB_TILE
    rows_f32 = pltpu.VMEM((SB_HEADS * SB_TILE, SB_TILE), F32)
    return pl.pallas_call(
        _sb_kernel,
        grid=(b, n_group),
        in_specs=[
            pl.BlockSpec((None, s, SB_LANES), lambda bi, g: (bi, 0, g),
                         pipeline_mode=pl.Buffered(1)),
            pl.BlockSpec((None, None, nk, SB_LANES, SB_TILE), lambda bi, g: (bi, g, 0, 0, 0),
                         pipeline_mode=pl.Buffered(1)),
            pl.BlockSpec((None, s, SB_LANES), lambda bi, g: (bi, 0, g),
                         pipeline_mode=pl.Buffered(1)),
        ],
        out_specs=pl.BlockSpec((None, s, SB_LANES), lambda bi, g: (bi, 0, g)),
        out_shape=jax.ShapeDtypeStruct((b, s, D_TOK), BF16),
        scratch_shapes=2 * [pltpu.VMEM((SB_HEADS, SB_TILE, SB_TILE), F32)]
                       + 2 * [rows_f32]
                       + 2 * [pltpu.VMEM((SB_HEADS * SB_TILE, 2 * SB_TILE), BF16)]
                       + 2 * [pltpu.VMEM((SB_TILE, SB_HEADS * SB_TILE), BF16)]
                       + [pltpu.VMEM((SB_TILE, SB_LANES), F32),
                          pltpu.VMEM((SB_HEADS * SB_TILE, 1), F32),
                        pltpu.VMEM((nk, SB_HEADS * SB_TILE, SB_LANES), BF16)],
        compiler_params=pltpu.CompilerParams(
            dimension_semantics=("arbitrary", "arbitrary"), vmem_limit_bytes=VMEM_LIMIT),
        name="stick_breaking",
    )(q, kt, v)


def kernel(x, mem, g_pre, g_post, g_mem, ffn1_gate, ffn1_up, ffn1_down, ffn2_gate, ffn2_up,
           ffn2_down, w_in_pool, pool_w, pool_scale, w_in_sb, w_mem_kv, w_out):
    depth, _, d = g_pre.shape
    assert x.shape[1] % max(SB_TILE, TOKEN_TILE) == 0 or x.shape[1] == SB_TILE
    params = {
        "g_pre": g_pre.reshape(depth * 3, 1, d),
        "g_post": g_post.reshape(depth * 3, 1, d),
        "ffn1_gate": ffn1_gate.astype(BF16), "ffn1_up": ffn1_up.astype(BF16),
        "ffn1_down": ffn1_down.astype(BF16),
        "ffn2_gate": ffn2_gate.astype(BF16), "ffn2_up": ffn2_up.astype(BF16),
        "ffn2_down": ffn2_down.astype(BF16),
        "w_in_pool": w_in_pool.astype(BF16),
        "pool_w": pool_w.astype(BF16),
        "pool_scale": pool_scale.reshape(pool_scale.shape[0], 1, D_TOK),
        "w_sb_q": w_in_sb[:, :, :D_TOK].astype(BF16),
        "w_sb_kt": jnp.swapaxes(w_in_sb[:, :, D_TOK:2 * D_TOK], 1, 2).astype(BF16),
        "w_sb_v": w_in_sb[:, :, 2 * D_TOK:3 * D_TOK].astype(BF16),
        "w_sb_qm": w_in_sb[:, :, 3 * D_TOK:].astype(BF16),
        "w_out": w_out.astype(BF16),
    }
    kmt, vm = _memkv_call(mem, g_mem,
                          jnp.swapaxes(w_mem_kv[:, :, :D_MEMH], 1, 2).astype(BF16),
                          w_mem_kv[:, :, D_MEMH:].astype(BF16))
    h = x
    for i in range(depth):
        mixer = "pool" if i % 2 == 0 else "sb"
        if mixer == "pool":
            h, tok, qm = _head_call(h, params, i, mixer, [(i, 1)])
        else:
            h, q, kt, v, qm = _head_call(h, params, i, mixer, [(i, 1)])
            tok = _sb_call(q, kt, v)
        h = _tail_call(h, tok, qm, kmt, vm, params, i, mixer, [(i, 2)])
    return h
```

```python
import functools

import jax
import jax.numpy as jnp
from jax import lax
from jax.experimental import pallas as pl
from jax.experimental.pallas import tpu as pltpu

F32 = jnp.float32
BF16 = jnp.bfloat16

EPS = 1e-6
HEAD_DIM = 64
SB_HEADS = 4
SB_LANES = SB_HEADS * HEAD_DIM
MASK_BIAS = -1e30
SB_ROWS = 32
SB_DOT_HEADS = 2
SB_BIG = 64.0
LOG2E = 1.4426950408889634
POOL_WINDOWS = (2, 4, 8, 16)
POOL_GROUP = 128
POOL_HALO = 16
D_TOK = 512
D_MEMH = 256
N_MEM_HEADS = D_MEMH // HEAD_DIM
SB_TILE = 256
FF_CHUNK = 512
TOKEN_TILE = 512
VMEM_LIMIT = 56 * 1024 * 1024


def _rms(x, g):
    ms = jnp.mean(x * x, axis=-1, keepdims=True)
    return x * lax.rsqrt(ms + EPS) * g


def _ffn_half_step(x, g_pre, wg_ref, wu_ref, wd_ref, g_post):
    xn = _rms(x, g_pre).astype(BF16)
    d_ff = wg_ref.shape[1]
    f = None
    for c in range(d_ff // FF_CHUNK):
        cs = slice(c * FF_CHUNK, (c + 1) * FF_CHUNK)
        gate = jnp.dot(xn, wg_ref[:, cs], preferred_element_type=F32)
        up = jnp.dot(xn, wu_ref[:, cs], preferred_element_type=F32)
        mid = (gate * (1.0 / (1.0 + jnp.exp(-gate))) * up).astype(BF16)
        part = jnp.dot(mid, wd_ref[cs, :], preferred_element_type=F32)
        f = part if f is None else f + part
    return x + 0.5 * _rms(f, g_post)


def _mem_attention(qm, kmt_ref, vm_ref):
    t = qm.shape[0]
    lane = lax.broadcasted_iota(jnp.int32, (t, D_MEMH), 1)
    kmt = kmt_ref[...]
    vm = vm_ref[...]
    out = jnp.zeros((t, D_MEMH), F32)
    for h in range(N_MEM_HEADS):
        sel = (lane >= h * HEAD_DIM) & (lane < (h + 1) * HEAD_DIM)
        qh = jnp.where(sel, qm, jnp.zeros_like(qm))
        s = jnp.dot(qh, kmt, preferred_element_type=F32) * (HEAD_DIM ** -0.5)
        e = jnp.exp(s - jnp.max(s, axis=-1, keepdims=True))
        p = (e / jnp.sum(e, axis=-1, keepdims=True)).astype(BF16)
        o = jnp.dot(p, vm, preferred_element_type=F32)
        out = jnp.where(sel, o, out)
    return out.astype(BF16)


def _memkv_kernel(mem_ref, g_ref, wkt_ref, wv_ref, kmt_ref, vm_ref):
    mn = _rms(mem_ref[...], g_ref[...]).astype(BF16)
    kmt = lax.dot_general(wkt_ref[...], mn, (((1,), (1,)), ((), ())),
                          preferred_element_type=F32)
    kmt_ref[...] = kmt.astype(BF16)
    vm_ref[...] = jnp.dot(mn, wv_ref[...], preferred_element_type=F32).astype(BF16)


def _memkv_call(mem, g_mem, w_kt, w_v):
    b, l, d = mem.shape
    depth = g_mem.shape[0]
    return pl.pallas_call(
        _memkv_kernel,
        grid=(b, depth),
        in_specs=[
            pl.BlockSpec((None, l, d), lambda bi, i: (bi, 0, 0)),
            pl.BlockSpec((None, 1, d), lambda bi, i: (i, 0, 0)),
            pl.BlockSpec((None, D_MEMH, d), lambda bi, i: (i, 0, 0)),
            pl.BlockSpec((None, d, D_MEMH), lambda bi, i: (i, 0, 0)),
        ],
        out_specs=[
            pl.BlockSpec((None, None, D_MEMH, l), lambda bi, i: (i, bi, 0, 0)),
            pl.BlockSpec((None, None, l, D_MEMH), lambda bi, i: (i, bi, 0, 0)),
        ],
        out_shape=[
            jax.ShapeDtypeStruct((depth, b, D_MEMH, l), BF16),
            jax.ShapeDtypeStruct((depth, b, l, D_MEMH), BF16),
        ],
        compiler_params=pltpu.CompilerParams(
            dimension_semantics=("arbitrary", "arbitrary"), vmem_limit_bytes=VMEM_LIMIT),
        name="memkv",
    )(mem, g_mem.reshape(depth, 1, d), w_kt, w_v)


def _const_spec(shape, layer):
    nd = len(shape)
    return pl.BlockSpec((None,) + tuple(shape), lambda bi, j: (layer,) + (0,) * nd,
                        pipeline_mode=pl.Buffered(1))


def _ffn_specs(d, d_ff, layer, which):
    g_idx = 3 * layer + (0 if which == 1 else 2)
    return [
        _const_spec((1, d), g_idx),
        _const_spec((d, d_ff), layer),
        _const_spec((d, d_ff), layer),
        _const_spec((d_ff, d), layer),
        _const_spec((1, d), g_idx),
    ]


def _head_kernel(*refs, mixer, n_ffn):
    it = iter(refs)
    h_ref = next(it)
    ffn_refs = [tuple(next(it) for _ in range(5)) for _ in range(n_ffn)]
    g_mix_ref = next(it)
    x = h_ref[...]
    if mixer == "pool":
        w_in_ref = next(it)
        h_out_ref, tok_ref, qm_ref = it
    else:
        w_q_ref, w_kt_ref, w_v_ref, w_qm_ref = (next(it) for _ in range(4))
        h_out_ref, q_ref, kt_ref, v_ref, qm_ref = it
    for g_pre, wg, wu, wd, g_post in ffn_refs:
        x = _ffn_half_step(x, g_pre[...], wg, wu, wd, g_post[...])
    h_out_ref[...] = x
    u = _rms(x, g_mix_ref[...]).astype(BF16)
    if mixer == "pool":
        proj = jnp.dot(u, w_in_ref[...], preferred_element_type=F32)
        tok_ref[...] = proj[:, :D_TOK]
        qm_ref[...] = proj[:, D_TOK:].astype(BF16)
    else:
        q = jnp.dot(u, w_q_ref[...], preferred_element_type=F32)
        q_ref[...] = (q * (HEAD_DIM ** -0.5)).astype(BF16)
        kt = lax.dot_general(w_kt_ref[...], u, (((1,), (1,)), ((), ())),
                             preferred_element_type=F32)
        for g in range(D_TOK // SB_LANES):
            for t in range(kt.shape[1] // SB_TILE):
                kt_ref[g, t] = kt[g * SB_LANES:(g + 1) * SB_LANES,
                                  t * SB_TILE:(t + 1) * SB_TILE].astype(BF16)
        v_ref[...] = jnp.dot(u, w_v_ref[...], preferred_element_type=F32).astype(BF16)
        qm_ref[...] = jnp.dot(u, w_qm_ref[...], preferred_element_type=F32).astype(BF16)


def _head_call(h, params, layer, mixer, ffn_list):
    b, s, d = h.shape
    tm = min(TOKEN_TILE, s)
    d_ff = params["ffn1_gate"].shape[2]
    tile = lambda w: pl.BlockSpec((None, tm, w), lambda bi, j: (bi, j, 0))
    in_specs = [tile(d)]
    args = [h]
    for (li, which) in ffn_list:
        in_specs += _ffn_specs(d, d_ff, li, which)
        pre = "ffn1" if which == 1 else "ffn2"
        args += [params["g_pre"], params[pre + "_gate"], params[pre + "_up"],
                 params[pre + "_down"], params["g_post"]]
    in_specs.append(_const_spec((1, d), 3 * layer + 1))
    args.append(params["g_pre"])
    j = layer // 2
    if mixer == "pool":
        in_specs.append(_const_spec((d, D_TOK + D_MEMH), j))
        args.append(params["w_in_pool"])
        out_specs = [tile(d), tile(D_TOK), tile(D_MEMH)]
        out_shape = [jax.ShapeDtypeStruct((b, s, d), F32),
                     jax.ShapeDtypeStruct((b, s, D_TOK), F32),
                     jax.ShapeDtypeStruct((b, s, D_MEMH), BF16)]
    else:
        in_specs += [_const_spec((d, D_TOK), j), _const_spec((D_TOK, d), j),
                     _const_spec((d, D_TOK), j), _const_spec((d, D_MEMH), j)]
        args += [params["w_sb_q"], params["w_sb_kt"], params["w_sb_v"], params["w_sb_qm"]]
        n_group = D_TOK // SB_LANES
        out_specs = [tile(d), tile(D_TOK),
                     pl.BlockSpec((None, n_group, tm // SB_TILE, SB_LANES, SB_TILE),
                                  lambda bi, j: (bi, 0, j, 0, 0)),
                     tile(D_TOK), tile(D_MEMH)]
        out_shape = [jax.ShapeDtypeStruct((b, s, d), F32),
                     jax.ShapeDtypeStruct((b, s, D_TOK), BF16),
                     jax.ShapeDtypeStruct((b, n_group, s // SB_TILE, SB_LANES, SB_TILE), BF16),
                     jax.ShapeDtypeStruct((b, s, D_TOK), BF16),
                     jax.ShapeDtypeStruct((b, s, D_MEMH), BF16)]
    return pl.pallas_call(
        functools.partial(_head_kernel, mixer=mixer, n_ffn=len(ffn_list)),
        grid=(b, s // tm),
        in_specs=in_specs, out_specs=out_specs, out_shape=out_shape,
        compiler_params=pltpu.CompilerParams(
            dimension_semantics=("arbitrary", "arbitrary"), vmem_limit_bytes=VMEM_LIMIT),
        name="head_" + mixer,
    )(*args)


def _pool_mixer(u, ext_ref, pw_ref, ps, first_tile, tile_start):
    t = u.shape[0]

    @pl.when(first_tile)
    def _():
        ext_ref[0:POOL_HALO, :] = jnp.zeros((POOL_HALO, D_TOK), F32)

    @pl.when(jnp.logical_not(first_tile))
    def _():
        ext_ref[0:POOL_HALO, :] = ext_ref[t:t + POOL_HALO, :]

    ext_ref[POOL_HALO:POOL_HALO + t, :] = u
    pos = tile_start + lax.broadcasted_iota(jnp.int32, (t, POOL_GROUP), 0)
    outs = []
    for gi, w in enumerate(POOL_WINDOWS):
        cs = slice(gi * POOL_GROUP, (gi + 1) * POOL_GROUP)
        win = ext_ref[POOL_HALO:POOL_HALO + t, cs]
        for back in range(1, w):
            win = win + ext_ref[POOL_HALO - back:POOL_HALO - back + t, cs]
        cnt = jnp.minimum(pos + 1, w).astype(F32)
        dlt = (win / cnt - u[:, cs]).astype(BF16)
        outs.append(jnp.dot(dlt, pw_ref[gi], preferred_element_type=F32))
    return (jnp.concatenate(outs, axis=-1) * ps).astype(BF16)


def _tail_kernel(*refs, mixer, n_ffn):
    it = iter(refs)
    h_ref, tok_ref, qm_ref, kmt_ref, vm_ref = (next(it) for _ in range(5))
    if mixer == "pool":
        pw_ref, ps_ref = next(it), next(it)
    w_out_ref, g_post_ref = next(it), next(it)
    ffn_refs = [tuple(next(it) for _ in range(5)) for _ in range(n_ffn)]
    h_out_ref = next(it)
    x = h_ref[...]
    if mixer == "pool":
        ext_ref = next(it)
        j = pl.program_id(1)
        tok = _pool_mixer(tok_ref[...], ext_ref, pw_ref, ps_ref[...], j == 0,
                          j * tok_ref.shape[0])
    else:
        tok = tok_ref[...]
    mo = _mem_attention(qm_ref[...], kmt_ref, vm_ref)
    cat = jnp.concatenate([tok, mo], axis=-1)
    mix = jnp.dot(cat, w_out_ref[...], preferred_element_type=F32)
    x = x + _rms(mix, g_post_ref[...])
    for g_pre, wg, wu, wd, g_post in ffn_refs:
        x = _ffn_half_step(x, g_pre[...], wg, wu, wd, g_post[...])
    h_out_ref[...] = x


def _tail_call(h, tok, qm, kmt, vm, params, layer, mixer, ffn_list):
    b, s, d = h.shape
    tm = min(TOKEN_TILE, s)
    d_ff = params["ffn1_gate"].shape[2]
    l = kmt.shape[-1]
    tile = lambda w: pl.BlockSpec((None, tm, w), lambda bi, j: (bi, j, 0))
    in_specs = [tile(d), tile(D_TOK), tile(D_MEMH),
                pl.BlockSpec((None, None, D_MEMH, l), lambda bi, j: (layer, bi, 0, 0)),
                pl.BlockSpec((None, None, l, D_MEMH), lambda bi, j: (layer, bi, 0, 0))]
    args = [h, tok, qm, kmt, vm]
    scratch = []
    if mixer == "pool":
        jl = layer // 2
        in_specs += [_const_spec((len(POOL_WINDOWS), POOL_GROUP, POOL_GROUP), jl),
                     _const_spec((1, D_TOK), jl)]
        args += [params["pool_w"], params["pool_scale"]]
        scratch = [pltpu.VMEM((POOL_HALO + tm, D_TOK), F32)]
    in_specs += [_const_spec((D_TOK + D_MEMH, d), layer), _const_spec((1, d), 3 * layer + 1)]
    args += [params["w_out"], params["g_post"]]
    for (li, which) in ffn_list:
        in_specs += _ffn_specs(d, d_ff, li, which)
        pre = "ffn1" if which == 1 else "ffn2"
        args += [params["g_pre"], params[pre + "_gate"], params[pre + "_up"],
                 params[pre + "_down"], params["g_post"]]
    return pl.pallas_call(
        functools.partial(_tail_kernel, mixer=mixer, n_ffn=len(ffn_list)),
        grid=(b, s // tm),
        in_specs=in_specs, out_specs=tile(d),
        out_shape=jax.ShapeDtypeStruct((b, s, d), F32),
        scratch_shapes=scratch,
        compiler_params=pltpu.CompilerParams(
            dimension_semantics=("arbitrary", "arbitrary"), vmem_limit_bytes=VMEM_LIMIT),
        name="tail_" + mixer,
    )(*args)


def _next_tile(qi, kj):
    wrap = kj == 0
    qn = qi + wrap.astype(jnp.int32)
    return qn, jnp.where(wrap, qn, kj - 1)


def _sb_kernel(q_ref, kt_ref, v_ref, o_ref, *scratch):
    z_refs, ls_refs, hl_refs, a_refs = (scratch[2 * i:2 * i + 2] for i in range(4))
    acc_ref, later_ref, qexp_ref, vexp_ref = scratch[8:]
    nq = kt_ref.shape[0]
    tq = tk = SB_TILE
    n_tiles = nq * (nq + 1) // 2
    n_stage = 4
    heads = range(SB_HEADS)

    row = lax.broadcasted_iota(jnp.int32, (tq, tk), 0)
    col = lax.broadcasted_iota(jnp.int32, (tq, tk), 1)
    ntri = jnp.where(row >= col, -1.0, 0.0).astype(BF16)
    ntri2 = jnp.concatenate([ntri, ntri], axis=0)
    for ref in scratch[:10]:
        ref[...] = jnp.zeros(ref.shape, ref.dtype)

    lane = lax.broadcasted_iota(jnp.int32, (tk, SB_LANES), 1)

    def expand_values(t, _):
        vt = v_ref[pl.ds(pl.multiple_of(t * tk, tk), tk), :]
        for h in heads:
            own = (lane >= h * HEAD_DIM) & (lane < (h + 1) * HEAD_DIM)
            vexp_ref[t, h * tk:(h + 1) * tk, :] = jnp.where(own, vt, jnp.zeros_like(vt))
        return 0

    lax.fori_loop(0, nq, expand_values, 0)

    def expand_queries(qi):
        qt = q_ref[pl.ds(pl.multiple_of(qi * tq, tq), tq), :]
        for h in heads:
            own = (lane >= h * HEAD_DIM) & (lane < (h + 1) * HEAD_DIM)
            qexp_ref[h * tq:(h + 1) * tq] = jnp.where(own, qt, jnp.zeros_like(qt))

    def step(cur, tiles):
        (q0, k0), _, _, (q3, k3) = tiles
        prev = 1 - cur
        z_ref, z_in = z_refs[cur], z_refs[prev]
        ls_ref, ls_in = ls_refs[cur], ls_refs[prev]
        hl_ref, hl_in = hl_refs[cur], hl_refs[prev]
        a_ref, a_in = a_refs[cur], a_refs[prev]

        pv = jnp.dot(a_in[...], vexp_ref[k3], preferred_element_type=F32)
        acc = jnp.where(q3 == k3, pv, acc_ref[...] + pv)
        acc_ref[...] = acc
        o_ref[pl.ds(pl.multiple_of(q3 * tq, tq), tq), :] = acc.astype(BF16)

        qc = jnp.minimum(q0, nq - 1)
        kc = jnp.minimum(k0, nq - 1)
        kt = kt_ref[kc]
        for g in range(0, SB_HEADS, SB_DOT_HEADS):
            grp = slice(g * tq, (g + SB_DOT_HEADS) * tq)
            z_ref[grp] = jnp.dot(qexp_ref[grp], kt, preferred_element_type=F32) * LOG2E

            logw = ls_in[grp] + jnp.dot(hl_in[grp], ntri2, preferred_element_type=F32)
            for h in range(g, g + SB_DOT_HEADS):
                a_ref[:, h * tk:(h + 1) * tk] = jnp.exp2(
                    logw[(h - g) * tq:(h - g + 1) * tq]).astype(BF16)

            for r in range(g * tq, (g + SB_DOT_HEADS) * tq, SB_ROWS):
                rows = slice(r, r + SB_ROWS)
                z = z_in[rows]
                nlf = jnp.where(z > SB_BIG, z, jnp.log2(1.0 + jnp.exp2(z)))
                later = later_ref[rows]
                ls_ref[rows] = z + later
                hi = nlf.astype(BF16)
                hl_ref[rows, 0:tk] = hi
                hl_ref[rows, tk:2 * tk] = (nlf - hi.astype(F32)).astype(BF16)
                later_ref[rows] = later - jnp.sum(nlf, axis=-1, keepdims=True)

        @pl.when(qc == kc)
        def _():
            bias = jnp.where(col < row, 0.0, MASK_BIAS)
            for h in heads:
                z_ref[h * tq:(h + 1) * tq] += bias
            later_ref[...] = jnp.zeros(later_ref.shape, F32)

        @pl.when(k0 == 0)
        def _():
            expand_queries(jnp.minimum(q0 + 1, nq - 1))

        return (_next_tile(q0, k0),) + tiles[:-1]

    expand_queries(0)
    zero = jnp.int32(0)
    n_iter = n_tiles + n_stage - 1
    tiles = lax.fori_loop(0, n_iter // 2, lambda _, c: step(1, step(0, c)),
                          ((zero, zero),) * n_stage)
    if n_iter % 2:
        step(0, tiles)


def _sb_call(q, kt, v):
    b, s, _ = q.shape
    n_group = D_TOK // SB_LANES
    nk = s // SB_TILE
    rows_f32 = pltpu.VMEM((SB_HEADS * SB_TILE, SB_TILE), F32)
    return pl.pallas_call(
        _sb_kernel,
        grid=(b, n_group),
        in_specs=[
            pl.BlockSpec((None, s, SB_LANES), lambda bi, g: (bi, 0, g),
                         pipeline_mode=pl.Buffered(1)),
            pl.BlockSpec((None, None, nk, SB_LANES, SB_TILE), lambda bi, g: (bi, g, 0, 0, 0),
                         pipeline_mode=pl.Buffered(1)),
            pl.BlockSpec((None, s, SB_LANES), lambda bi, g: (bi, 0, g),
                         pipeline_mode=pl.Buffered(1)),
        ],
        out_specs=pl.BlockSpec((None, s, SB_LANES), lambda bi, g: (bi, 0, g)),
        out_shape=jax.ShapeDtypeStruct((b, s, D_TOK), BF16),
        scratch_shapes=2 * [rows_f32]
                       + 2 * [rows_f32]
                       + 2 * [pltpu.VMEM((SB_HEADS * SB_TILE, 2 * SB_TILE), BF16)]
                       + 2 * [pltpu.VMEM((SB_TILE, SB_HEADS * SB_TILE), BF16)]
                       + [pltpu.VMEM((SB_TILE, SB_LANES), F32),
                          pltpu.VMEM((SB_HEADS * SB_TILE, 1), F32),
                          pltpu.VMEM((SB_HEADS * SB_TILE, SB_LANES), BF16),
                          pltpu.VMEM((nk, SB_HEADS * SB_TILE, SB_LANES), BF16)],
        compiler_params=pltpu.CompilerParams(
            dimension_semantics=("arbitrary", "arbitrary"), vmem_limit_bytes=VMEM_LIMIT),
        name="stick_breaking",
    )(q, kt, v)


def kernel(x, mem, g_pre, g_post, g_mem, ffn1_gate, ffn1_up, ffn1_down, ffn2_gate, ffn2_up,
           ffn2_down, w_in_pool, pool_w, pool_scale, w_in_sb, w_mem_kv, w_out):
    depth, _, d = g_pre.shape
    assert x.shape[1] % max(SB_TILE, TOKEN_TILE) == 0 or x.shape[1] == SB_TILE
    params = {
        "g_pre": g_pre.reshape(depth * 3, 1, d),
        "g_post": g_post.reshape(depth * 3, 1, d),
        "ffn1_gate": ffn1_gate.astype(BF16), "ffn1_up": ffn1_up.astype(BF16),
        "ffn1_down": ffn1_down.astype(BF16),
        "ffn2_gate": ffn2_gate.astype(BF16), "ffn2_up": ffn2_up.astype(BF16),
        "ffn2_down": ffn2_down.astype(BF16),
        "w_in_pool": w_in_pool.astype(BF16),
        "pool_w": pool_w.astype(BF16),
        "pool_scale": pool_scale.reshape(pool_scale.shape[0], 1, D_TOK),
        "w_sb_q": w_in_sb[:, :, :D_TOK].astype(BF16),
        "w_sb_kt": jnp.swapaxes(w_in_sb[:, :, D_TOK:2 * D_TOK], 1, 2).astype(BF16),
        "w_sb_v": w_in_sb[:, :, 2 * D_TOK:3 * D_TOK].astype(BF16),
        "w_sb_qm": w_in_sb[:, :, 3 * D_TOK:].astype(BF16),
        "w_out": w_out.astype(BF16),
    }
    kmt, vm = _memkv_call(mem, g_mem,
                          jnp.swapaxes(w_mem_kv[:, :, :D_MEMH], 1, 2).astype(BF16),
                          w_mem_kv[:, :, D_MEMH:].astype(BF16))
    h = x
    for i in range(depth):
        mixer = "pool" if i % 2 == 0 else "sb"
        if mixer == "pool":
            h, tok, qm = _head_call(h, params, i, mixer, [(i, 1)])
        else:
            h, q, kt, v, qm = _head_call(h, params, i, mixer, [(i, 1)])
            tok = _sb_call(q, kt, v)
        h = _tail_call(h, tok, qm, kmt, vm, params, i, mixer, [(i, 2)])
    return h
```

```python
import functools

import jax
import jax.numpy as jnp
from jax import lax
from jax.experimental import pallas as pl
from jax.experimental.pallas import tpu as pltpu

F32 = jnp.float32
BF16 = jnp.bfloat16

EPS = 1e-6
HEAD_DIM = 64
SB_HEADS = 4
SB_LANES = SB_HEADS * HEAD_DIM
MASK_BIAS = -1e30
SB_ROWS = 32
SB_DOT_HEADS = 2
SB_BIG = 64.0
LOG2E = 1.4426950408889634
POOL_WINDOWS = (2, 4, 8, 16)
POOL_GROUP = 128
POOL_HALO = 16
D_TOK = 512
D_MEMH = 256
N_MEM_HEADS = D_MEMH // HEAD_DIM
SB_TILE = 256
FF_CHUNK = 512
ROW_SLICE = 512
TOKEN_TILE = 1024
VMEM_LIMIT = 56 * 1024 * 1024


def _rms(x, g):
    ms = jnp.mean(x * x, axis=-1, keepdims=True)
    return x * lax.rsqrt(ms + EPS) * g


def _row_slices(t):
    size = min(t, ROW_SLICE)
    return [slice(r, r + size) for r in range(0, t, size)]


def _ffn_half_step(xs, g_pre, wg_ref, wu_ref, wd_ref, g_post):
    xns = [_rms(x, g_pre).astype(BF16) for x in xs]
    d_ff = wg_ref.shape[1]
    fs = [None] * len(xs)
    for c in range(d_ff // FF_CHUNK):
        cs = slice(c * FF_CHUNK, (c + 1) * FF_CHUNK)
        for i, xn in enumerate(xns):
            gate = jnp.dot(xn, wg_ref[:, cs], preferred_element_type=F32)
            up = jnp.dot(xn, wu_ref[:, cs], preferred_element_type=F32)
            mid = (gate * (1.0 / (1.0 + jnp.exp(-gate))) * up).astype(BF16)
            part = jnp.dot(mid, wd_ref[cs, :], preferred_element_type=F32)
            fs[i] = part if fs[i] is None else fs[i] + part
    return [x + 0.5 * _rms(f, g_post) for x, f in zip(xs, fs)]


def _mem_attention(qm, kmt_ref, vm_ref):
    t = qm.shape[0]
    lane = lax.broadcasted_iota(jnp.int32, (t, D_MEMH), 1)
    kmt = kmt_ref[...]
    vm = vm_ref[...]
    out = jnp.zeros((t, D_MEMH), F32)
    for h in range(N_MEM_HEADS):
        sel = (lane >= h * HEAD_DIM) & (lane < (h + 1) * HEAD_DIM)
        qh = jnp.where(sel, qm, jnp.zeros_like(qm))
        s = jnp.dot(qh, kmt, preferred_element_type=F32) * (HEAD_DIM ** -0.5)
        e = jnp.exp(s - jnp.max(s, axis=-1, keepdims=True))
        p = (e / jnp.sum(e, axis=-1, keepdims=True)).astype(BF16)
        o = jnp.dot(p, vm, preferred_element_type=F32)
        out = jnp.where(sel, o, out)
    return out.astype(BF16)


def _memkv_kernel(mem_ref, g_ref, wkt_ref, wv_ref, kmt_ref, vm_ref):
    mn = _rms(mem_ref[...], g_ref[...]).astype(BF16)
    kmt = lax.dot_general(wkt_ref[...], mn, (((1,), (1,)), ((), ())),
                          preferred_element_type=F32)
    kmt_ref[...] = kmt.astype(BF16)
    vm_ref[...] = jnp.dot(mn, wv_ref[...], preferred_element_type=F32).astype(BF16)


def _memkv_call(mem, g_mem, w_kt, w_v):
    b, l, d = mem.shape
    depth = g_mem.shape[0]
    return pl.pallas_call(
        _memkv_kernel,
        grid=(b, depth),
        in_specs=[
            pl.BlockSpec((None, l, d), lambda bi, i: (bi, 0, 0)),
            pl.BlockSpec((None, 1, d), lambda bi, i: (i, 0, 0)),
            pl.BlockSpec((None, D_MEMH, d), lambda bi, i: (i, 0, 0)),
            pl.BlockSpec((None, d, D_MEMH), lambda bi, i: (i, 0, 0)),
        ],
        out_specs=[
            pl.BlockSpec((None, None, D_MEMH, l), lambda bi, i: (i, bi, 0, 0)),
            pl.BlockSpec((None, None, l, D_MEMH), lambda bi, i: (i, bi, 0, 0)),
        ],
        out_shape=[
            jax.ShapeDtypeStruct((depth, b, D_MEMH, l), BF16),
            jax.ShapeDtypeStruct((depth, b, l, D_MEMH), BF16),
        ],
        compiler_params=pltpu.CompilerParams(
            dimension_semantics=("arbitrary", "arbitrary"), vmem_limit_bytes=VMEM_LIMIT),
        name="memkv",
    )(mem, g_mem.reshape(depth, 1, d), w_kt, w_v)


def _const_spec(shape, layer):
    nd = len(shape)
    return pl.BlockSpec((None,) + tuple(shape), lambda bi, j: (layer,) + (0,) * nd,
                        pipeline_mode=pl.Buffered(1))


def _ffn_specs(d, d_ff, layer, which):
    g_idx = 3 * layer + (0 if which == 1 else 2)
    return [
        _const_spec((1, d), g_idx),
        _const_spec((d, d_ff), layer),
        _const_spec((d, d_ff), layer),
        _const_spec((d_ff, d), layer),
        _const_spec((1, d), g_idx),
    ]


def _head_kernel(*refs, mixer, n_ffn):
    it = iter(refs)
    h_ref = next(it)
    ffn_refs = [tuple(next(it) for _ in range(5)) for _ in range(n_ffn)]
    g_mix_ref = next(it)
    if mixer == "pool":
        w_in_ref = next(it)
        h_out_ref, tok_ref, qm_ref = it
    else:
        w_q_ref, w_kt_ref, w_v_ref, w_qm_ref = (next(it) for _ in range(4))
        h_out_ref, q_ref, kt_ref, v_ref, qm_ref = it
    slices = _row_slices(h_ref.shape[0])
    xs = [h_ref[rows] for rows in slices]
    for g_pre, wg, wu, wd, g_post in ffn_refs:
        xs = _ffn_half_step(xs, g_pre[...], wg, wu, wd, g_post[...])
    for rows, x in zip(slices, xs):
        h_out_ref[rows] = x
        u = _rms(x, g_mix_ref[...]).astype(BF16)
        if mixer == "pool":
            proj = jnp.dot(u, w_in_ref[...], preferred_element_type=F32)
            tok_ref[rows] = proj[:, :D_TOK]
            qm_ref[rows] = proj[:, D_TOK:].astype(BF16)
        else:
            q = jnp.dot(u, w_q_ref[...], preferred_element_type=F32)
            q_ref[rows] = (q * (HEAD_DIM ** -0.5)).astype(BF16)
            kt = lax.dot_general(w_kt_ref[...], u, (((1,), (1,)), ((), ())),
                                 preferred_element_type=F32)
            for g in range(D_TOK // SB_LANES):
                for t in range(kt.shape[1] // SB_TILE):
                    kt_ref[g, rows.start // SB_TILE + t] = kt[
                        g * SB_LANES:(g + 1) * SB_LANES,
                        t * SB_TILE:(t + 1) * SB_TILE].astype(BF16)
            v_ref[rows] = jnp.dot(u, w_v_ref[...], preferred_element_type=F32).astype(BF16)
            qm_ref[rows] = jnp.dot(u, w_qm_ref[...], preferred_element_type=F32).astype(BF16)


def _head_call(h, params, layer, mixer, ffn_list):
    b, s, d = h.shape
    tm = min(TOKEN_TILE, s)
    d_ff = params["ffn1_gate"].shape[2]
    tile = lambda w: pl.BlockSpec((None, tm, w), lambda bi, j: (bi, j, 0))
    in_specs = [tile(d)]
    args = [h]
    for (li, which) in ffn_list:
        in_specs += _ffn_specs(d, d_ff, li, which)
        pre = "ffn1" if which == 1 else "ffn2"
        args += [params["g_pre"], params[pre + "_gate"], params[pre + "_up"],
                 params[pre + "_down"], params["g_post"]]
    in_specs.append(_const_spec((1, d), 3 * layer + 1))
    args.append(params["g_pre"])
    j = layer // 2
    if mixer == "pool":
        in_specs.append(_const_spec((d, D_TOK + D_MEMH), j))
        args.append(params["w_in_pool"])
        out_specs = [tile(d), tile(D_TOK), tile(D_MEMH)]
        out_shape = [jax.ShapeDtypeStruct((b, s, d), F32),
                     jax.ShapeDtypeStruct((b, s, D_TOK), F32),
                     jax.ShapeDtypeStruct((b, s, D_MEMH), BF16)]
    else:
        in_specs += [_const_spec((d, D_TOK), j), _const_spec((D_TOK, d), j),
                     _const_spec((d, D_TOK), j), _const_spec((d, D_MEMH), j)]
        args += [params["w_sb_q"], params["w_sb_kt"], params["w_sb_v"], params["w_sb_qm"]]
        n_group = D_TOK // SB_LANES
        out_specs = [tile(d), tile(D_TOK),
                     pl.BlockSpec((None, n_group, tm // SB_TILE, SB_LANES, SB_TILE),
                                  lambda bi, j: (bi, 0, j, 0, 0)),
                     tile(D_TOK), tile(D_MEMH)]
        out_shape = [jax.ShapeDtypeStruct((b, s, d), F32),
                     jax.ShapeDtypeStruct((b, s, D_TOK), BF16),
                     jax.ShapeDtypeStruct((b, n_group, s // SB_TILE, SB_LANES, SB_TILE), BF16),
                     jax.ShapeDtypeStruct((b, s, D_TOK), BF16),
                     jax.ShapeDtypeStruct((b, s, D_MEMH), BF16)]
    return pl.pallas_call(
        functools.partial(_head_kernel, mixer=mixer, n_ffn=len(ffn_list)),
        grid=(b, s // tm),
        in_specs=in_specs, out_specs=out_specs, out_shape=out_shape,
        compiler_params=pltpu.CompilerParams(
            dimension_semantics=("arbitrary", "arbitrary"), vmem_limit_bytes=VMEM_LIMIT),
        name="head_" + mixer,
    )(*args)


def _pool_mixer(u, ext_ref, pw_ref, ps, first_tile, tile_start):
    t = u.shape[0]

    @pl.when(first_tile)
    def _():
        ext_ref[0:POOL_HALO, :] = jnp.zeros((POOL_HALO, D_TOK), F32)

    @pl.when(jnp.logical_not(first_tile))
    def _():
        ext_ref[0:POOL_HALO, :] = ext_ref[t:t + POOL_HALO, :]

    ext_ref[POOL_HALO:POOL_HALO + t, :] = u
    pos = tile_start + lax.broadcasted_iota(jnp.int32, (t, POOL_GROUP), 0)
    outs = []
    for gi, w in enumerate(POOL_WINDOWS):
        cs = slice(gi * POOL_GROUP, (gi + 1) * POOL_GROUP)
        win = ext_ref[POOL_HALO:POOL_HALO + t, cs]
        for back in range(1, w):
            win = win + ext_ref[POOL_HALO - back:POOL_HALO - back + t, cs]
        cnt = jnp.minimum(pos + 1, w).astype(F32)
        dlt = (win / cnt - u[:, cs]).astype(BF16)
        outs.append(jnp.dot(dlt, pw_ref[gi], preferred_element_type=F32))
    return (jnp.concatenate(outs, axis=-1) * ps).astype(BF16)


def _tail_kernel(*refs, mixer, n_ffn):
    it = iter(refs)
    h_ref, tok_ref, qm_ref, kmt_ref, vm_ref = (next(it) for _ in range(5))
    if mixer == "pool":
        pw_ref, ps_ref = next(it), next(it)
    w_out_ref, g_post_ref = next(it), next(it)
    ffn_refs = [tuple(next(it) for _ in range(5)) for _ in range(n_ffn)]
    h_out_ref = next(it)
    if mixer == "pool":
        ext_ref = next(it)
        j = pl.program_id(1)
        tok = _pool_mixer(tok_ref[...], ext_ref, pw_ref, ps_ref[...], j == 0,
                          j * tok_ref.shape[0])
    slices = _row_slices(h_ref.shape[0])
    xs = []
    for rows in slices:
        mo = _mem_attention(qm_ref[rows], kmt_ref, vm_ref)
        cat = jnp.concatenate([tok[rows] if mixer == "pool" else tok_ref[rows], mo], axis=-1)
        mix = jnp.dot(cat, w_out_ref[...], preferred_element_type=F32)
        xs.append(h_ref[rows] + _rms(mix, g_post_ref[...]))
    for g_pre, wg, wu, wd, g_post in ffn_refs:
        xs = _ffn_half_step(xs, g_pre[...], wg, wu, wd, g_post[...])
    for rows, x in zip(slices, xs):
        h_out_ref[rows] = x


def _tail_call(h, tok, qm, kmt, vm, params, layer, mixer, ffn_list):
    b, s, d = h.shape
    tm = min(TOKEN_TILE, s)
    d_ff = params["ffn1_gate"].shape[2]
    l = kmt.shape[-1]
    tile = lambda w: pl.BlockSpec((None, tm, w), lambda bi, j: (bi, j, 0))
    in_specs = [tile(d), tile(D_TOK), tile(D_MEMH),
                pl.BlockSpec((None, None, D_MEMH, l), lambda bi, j: (layer, bi, 0, 0)),
                pl.BlockSpec((None, None, l, D_MEMH), lambda bi, j: (layer, bi, 0, 0))]
    args = [h, tok, qm, kmt, vm]
    scratch = []
    if mixer == "pool":
        jl = layer // 2
        in_specs += [_const_spec((len(POOL_WINDOWS), POOL_GROUP, POOL_GROUP), jl),
                     _const_spec((1, D_TOK), jl)]
        args += [params["pool_w"], params["pool_scale"]]
        scratch = [pltpu.VMEM((POOL_HALO + tm, D_TOK), F32)]
    in_specs += [_const_spec((D_TOK + D_MEMH, d), layer), _const_spec((1, d), 3 * layer + 1)]
    args += [params["w_out"], params["g_post"]]
    for (li, which) in ffn_list:
        in_specs += _ffn_specs(d, d_ff, li, which)
        pre = "ffn1" if which == 1 else "ffn2"
        args += [params["g_pre"], params[pre + "_gate"], params[pre + "_up"],
                 params[pre + "_down"], params["g_post"]]
    return pl.pallas_call(
        functools.partial(_tail_kernel, mixer=mixer, n_ffn=len(ffn_list)),
        grid=(b, s // tm),
        in_specs=in_specs, out_specs=tile(d),
        out_shape=jax.ShapeDtypeStruct((b, s, d), F32),
        scratch_shapes=scratch,
        compiler_params=pltpu.CompilerParams(
            dimension_semantics=("arbitrary", "arbitrary"), vmem_limit_bytes=VMEM_LIMIT),
        name="tail_" + mixer,
    )(*args)


def _next_tile(qi, kj):
    wrap = kj == 0
    qn = qi + wrap.astype(jnp.int32)
    return qn, jnp.where(wrap, qn, kj - 1)


def _sb_kernel(q_ref, kt_ref, v_ref, o_ref, *scratch):
    z_refs, ls_refs, hl_refs, a_refs = (scratch[2 * i:2 * i + 2] for i in range(4))
    acc_ref, later_ref, qexp_ref, vexp_ref = scratch[8:]
    nq = kt_ref.shape[0]
    tq = tk = SB_TILE
    n_tiles = nq * (nq + 1) // 2
    n_stage = 4
    heads = range(SB_HEADS)

    row = lax.broadcasted_iota(jnp.int32, (tq, tk), 0)
    col = lax.broadcasted_iota(jnp.int32, (tq, tk), 1)
    ntri = jnp.where(row >= col, -1.0, 0.0).astype(BF16)
    ntri2 = jnp.concatenate([ntri, ntri], axis=0)
    for ref in scratch[:10]:
        ref[...] = jnp.zeros(ref.shape, ref.dtype)

    lane = lax.broadcasted_iota(jnp.int32, (tk, SB_LANES), 1)

    def expand_values(t, _):
        vt = v_ref[pl.ds(pl.multiple_of(t * tk, tk), tk), :]
        for h in heads:
            own = (lane >= h * HEAD_DIM) & (lane < (h + 1) * HEAD_DIM)
            vexp_ref[t, h * tk:(h + 1) * tk, :] = jnp.where(own, vt, jnp.zeros_like(vt))
        return 0

    lax.fori_loop(0, nq, expand_values, 0)

    def expand_queries(qi):
        qt = q_ref[pl.ds(pl.multiple_of(qi * tq, tq), tq), :]
        for h in heads:
            own = (lane >= h * HEAD_DIM) & (lane < (h + 1) * HEAD_DIM)
            qexp_ref[h * tq:(h + 1) * tq] = jnp.where(own, qt, jnp.zeros_like(qt))

    def step(cur, tiles):
        (q0, k0), _, _, (q3, k3) = tiles
        prev = 1 - cur
        z_ref, z_in = z_refs[cur], z_refs[prev]
        ls_ref, ls_in = ls_refs[cur], ls_refs[prev]
        hl_ref, hl_in = hl_refs[cur], hl_refs[prev]
        a_ref, a_in = a_refs[cur], a_refs[prev]

        pv = jnp.dot(a_in[...], vexp_ref[k3], preferred_element_type=F32)
        acc = jnp.where(q3 == k3, pv, acc_ref[...] + pv)
        acc_ref[...] = acc
        o_ref[pl.ds(pl.multiple_of(q3 * tq, tq), tq), :] = acc.astype(BF16)

        qc = jnp.minimum(q0, nq - 1)
        kc = jnp.minimum(k0, nq - 1)
        kt = kt_ref[kc]
        for g in range(0, SB_HEADS, SB_DOT_HEADS):
            grp = slice(g * tq, (g + SB_DOT_HEADS) * tq)
            z_ref[grp] = jnp.dot(qexp_ref[grp], kt, preferred_element_type=F32) * LOG2E

            logw = ls_in[grp] + jnp.dot(hl_in[grp], ntri2, preferred_element_type=F32)
            for h in range(g, g + SB_DOT_HEADS):
                a_ref[:, h * tk:(h + 1) * tk] = jnp.exp2(
                    logw[(h - g) * tq:(h - g + 1) * tq]).astype(BF16)

            for r in range(g * tq, (g + SB_DOT_HEADS) * tq, SB_ROWS):
                rows = slice(r, r + SB_ROWS)
                z = z_in[rows]
                nlf = jnp.where(z > SB_BIG, z, jnp.log2(1.0 + jnp.exp2(z)))
                later = later_ref[rows]
                ls_ref[rows] = z + later
                hi = nlf.astype(BF16)
                hl_ref[rows, 0:tk] = hi
                hl_ref[rows, tk:2 * tk] = (nlf - hi.astype(F32)).astype(BF16)
                later_ref[rows] = later - jnp.sum(nlf, axis=-1, keepdims=True)

        @pl.when(qc == kc)
        def _():
            bias = jnp.where(col < row, 0.0, MASK_BIAS)
            for h in heads:
                z_ref[h * tq:(h + 1) * tq] += bias
            later_ref[...] = jnp.zeros(later_ref.shape, F32)

        @pl.when(k0 == 0)
        def _():
            expand_queries(jnp.minimum(q0 + 1, nq - 1))

        return (_next_tile(q0, k0),) + tiles[:-1]

    expand_queries(0)
    zero = jnp.int32(0)
    n_iter = n_tiles + n_stage - 1
    tiles = lax.fori_loop(0, n_iter // 2, lambda _, c: step(1, step(0, c)),
                          ((zero, zero),) * n_stage)
    if n_iter % 2:
        step(0, tiles)


def _sb_call(q, kt, v):
    b, s, _ = q.shape
    n_group = D_TOK // SB_LANES
    nk = s // SB_TILE
    rows_f32 = pltpu.VMEM((SB_HEADS * SB_TILE, SB_TILE), F32)
    return pl.pallas_call(
        _sb_kernel,
        grid=(b, n_group),
        in_specs=[
            pl.BlockSpec((None, s, SB_LANES), lambda bi, g: (bi, 0, g),
                         pipeline_mode=pl.Buffered(1)),
            pl.BlockSpec((None, None, nk, SB_LANES, SB_TILE), lambda bi, g: (bi, g, 0, 0, 0),
                         pipeline_mode=pl.Buffered(1)),
            pl.BlockSpec((None, s, SB_LANES), lambda bi, g: (bi, 0, g),
                         pipeline_mode=pl.Buffered(1)),
        ],
        out_specs=pl.BlockSpec((None, s, SB_LANES), lambda bi, g: (bi, 0, g)),
        out_shape=jax.ShapeDtypeStruct((b, s, D_TOK), BF16),
        scratch_shapes=2 * [rows_f32]
                       + 2 * [rows_f32]
                       + 2 * [pltpu.VMEM((SB_HEADS * SB_TILE, 2 * SB_TILE), BF16)]
                       + 2 * [pltpu.VMEM((SB_TILE, SB_HEADS * SB_TILE), BF16)]
                       + [pltpu.VMEM((SB_TILE, SB_LANES), F32),
                          pltpu.VMEM((SB_HEADS * SB_TILE, 1), F32),
                          pltpu.VMEM((SB_HEADS * SB_TILE, SB_LANES), BF16),
                          pltpu.VMEM((nk, SB_HEADS * SB_TILE, SB_LANES), BF16)],
        compiler_params=pltpu.CompilerParams(
            dimension_semantics=("arbitrary", "arbitrary"), vmem_limit_bytes=VMEM_LIMIT),
        name="stick_breaking",
    )(q, kt, v)


def kernel(x, mem, g_pre, g_post, g_mem, ffn1_gate, ffn1_up, ffn1_down, ffn2_gate, ffn2_up,
           ffn2_down, w_in_pool, pool_w, pool_scale, w_in_sb, w_mem_kv, w_out):
    depth, _, d = g_pre.shape
    assert x.shape[1] % max(SB_TILE, TOKEN_TILE) == 0 or x.shape[1] == SB_TILE
    params = {
        "g_pre": g_pre.reshape(depth * 3, 1, d),
        "g_post": g_post.reshape(depth * 3, 1, d),
        "ffn1_gate": ffn1_gate.astype(BF16), "ffn1_up": ffn1_up.astype(BF16),
        "ffn1_down": ffn1_down.astype(BF16),
        "ffn2_gate": ffn2_gate.astype(BF16), "ffn2_up": ffn2_up.astype(BF16),
        "ffn2_down": ffn2_down.astype(BF16),
        "w_in_pool": w_in_pool.astype(BF16),
        "pool_w": pool_w.astype(BF16),
        "pool_scale": pool_scale.reshape(pool_scale.shape[0], 1, D_TOK),
        "w_sb_q": w_in_sb[:, :, :D_TOK].astype(BF16),
        "w_sb_kt": jnp.swapaxes(w_in_sb[:, :, D_TOK:2 * D_TOK], 1, 2).astype(BF16),
        "w_sb_v": w_in_sb[:, :, 2 * D_TOK:3 * D_TOK].astype(BF16),
        "w_sb_qm": w_in_sb[:, :, 3 * D_TOK:].astype(BF16),
        "w_out": w_out.astype(BF16),
    }
    kmt, vm = _memkv_call(mem, g_mem,
                          jnp.swapaxes(w_mem_kv[:, :, :D_MEMH], 1, 2).astype(BF16),
                          w_mem_kv[:, :, D_MEMH:].astype(BF16))
    h = x
    for i in range(depth):
        mixer = "pool" if i % 2 == 0 else "sb"
        if mixer == "pool":
            h, tok, qm = _head_call(h, params, i, mixer, [(i, 1)])
        else:
            h, q, kt, v, qm = _head_call(h, params, i, mixer, [(i, 1)])
            tok = _sb_call(q, kt, v)
        h = _tail_call(h, tok, qm, kmt, vm, params, i, mixer, [(i, 2)])
    return h
```

```python
import functools

import jax
import jax.numpy as jnp
from jax import lax
from jax.experimental import pallas as pl
from jax.experimental.pallas import tpu as pltpu

F32 = jnp.float32
BF16 = jnp.bfloat16

EPS = 1e-6
HEAD_DIM = 64
SB_HEADS = 4
SB_LANES = SB_HEADS * HEAD_DIM
MASK_BIAS = -1e30
SB_ROWS = 32
SB_DOT_HEADS = 2
SB_BIG = 64.0
SB_SKIP = 160.0
LOG2E = 1.4426950408889634
POOL_WINDOWS = (2, 4, 8, 16)
POOL_GROUP = 128
POOL_HALO = 16
D_TOK = 512
D_MEMH = 256
N_MEM_HEADS = D_MEMH // HEAD_DIM
SB_TILE = 256
SB_TQ = 256
FF_CHUNK = 512
ROW_SLICE = 512
TOKEN_TILE = 1024
VMEM_LIMIT = 56 * 1024 * 1024


def _rms(x, g):
    ms = jnp.mean(x * x, axis=-1, keepdims=True)
    return x * lax.rsqrt(ms + EPS) * g


def _row_slices(t):
    size = min(t, ROW_SLICE)
    return [slice(r, r + size) for r in range(0, t, size)]


def _ffn_half_step(xs, g_pre, wg_ref, wu_ref, wd_ref, g_post):
    xns = [_rms(x, g_pre).astype(BF16) for x in xs]
    d_ff = wg_ref.shape[1]
    fs = [None] * len(xs)
    for c in range(d_ff // FF_CHUNK):
        cs = slice(c * FF_CHUNK, (c + 1) * FF_CHUNK)
        for i, xn in enumerate(xns):
            gate = jnp.dot(xn, wg_ref[:, cs], preferred_element_type=F32)
            up = jnp.dot(xn, wu_ref[:, cs], preferred_element_type=F32)
            mid = (gate * (1.0 / (1.0 + jnp.exp(-gate))) * up).astype(BF16)
            part = jnp.dot(mid, wd_ref[cs, :], preferred_element_type=F32)
            fs[i] = part if fs[i] is None else fs[i] + part
    return [x + 0.5 * _rms(f, g_post) for x, f in zip(xs, fs)]


def _mem_attention(qm, kmt_ref, vm_ref):
    t = qm.shape[0]
    lane = lax.broadcasted_iota(jnp.int32, (t, D_MEMH), 1)
    kmt = kmt_ref[...]
    vm = vm_ref[...]
    out = jnp.zeros((t, D_MEMH), F32)
    for h in range(N_MEM_HEADS):
        sel = (lane >= h * HEAD_DIM) & (lane < (h + 1) * HEAD_DIM)
        qh = jnp.where(sel, qm, jnp.zeros_like(qm))
        s = jnp.dot(qh, kmt, preferred_element_type=F32) * (HEAD_DIM ** -0.5)
        e = jnp.exp(s - jnp.max(s, axis=-1, keepdims=True))
        p = (e / jnp.sum(e, axis=-1, keepdims=True)).astype(BF16)
        o = jnp.dot(p, vm, preferred_element_type=F32)
        out = jnp.where(sel, o, out)
    return out.astype(BF16)


def _memkv_kernel(mem_ref, g_ref, wkt_ref, wv_ref, kmt_ref, vm_ref):
    mn = _rms(mem_ref[...], g_ref[...]).astype(BF16)
    kmt = lax.dot_general(wkt_ref[...], mn, (((1,), (1,)), ((), ())),
                          preferred_element_type=F32)
    kmt_ref[...] = kmt.astype(BF16)
    vm_ref[...] = jnp.dot(mn, wv_ref[...], preferred_element_type=F32).astype(BF16)


def _memkv_call(mem, g_mem, w_kt, w_v):
    b, l, d = mem.shape
    depth = g_mem.shape[0]
    return pl.pallas_call(
        _memkv_kernel,
        grid=(b, depth),
        in_specs=[
            pl.BlockSpec((None, l, d), lambda bi, i: (bi, 0, 0)),
            pl.BlockSpec((None, 1, d), lambda bi, i: (i, 0, 0)),
            pl.BlockSpec((None, D_MEMH, d), lambda bi, i: (i, 0, 0)),
            pl.BlockSpec((None, d, D_MEMH), lambda bi, i: (i, 0, 0)),
        ],
        out_specs=[
            pl.BlockSpec((None, None, D_MEMH, l), lambda bi, i: (i, bi, 0, 0)),
            pl.BlockSpec((None, None, l, D_MEMH), lambda bi, i: (i, bi, 0, 0)),
        ],
        out_shape=[
            jax.ShapeDtypeStruct((depth, b, D_MEMH, l), BF16),
            jax.ShapeDtypeStruct((depth, b, l, D_MEMH), BF16),
        ],
        compiler_params=pltpu.CompilerParams(
            dimension_semantics=("arbitrary", "arbitrary"), vmem_limit_bytes=VMEM_LIMIT),
        name="memkv",
    )(mem, g_mem.reshape(depth, 1, d), w_kt, w_v)


def _const_spec(shape, layer):
    nd = len(shape)
    return pl.BlockSpec((None,) + tuple(shape), lambda bi, j: (layer,) + (0,) * nd,
                        pipeline_mode=pl.Buffered(1))


def _ffn_specs(d, d_ff, layer, which):
    g_idx = 3 * layer + (0 if which == 1 else 2)
    return [
        _const_spec((1, d), g_idx),
        _const_spec((d, d_ff), layer),
        _const_spec((d, d_ff), layer),
        _const_spec((d_ff, d), layer),
        _const_spec((1, d), g_idx),
    ]


def _head_kernel(*refs, mixer, n_ffn):
    it = iter(refs)
    h_ref = next(it)
    ffn_refs = [tuple(next(it) for _ in range(5)) for _ in range(n_ffn)]
    g_mix_ref = next(it)
    if mixer == "pool":
        w_in_ref = next(it)
        h_out_ref, tok_ref, qm_ref = it
    else:
        w_q_ref, w_kt_ref, w_v_ref, w_qm_ref = (next(it) for _ in range(4))
        h_out_ref, q_ref, kt_ref, v_ref, qm_ref = it
    slices = _row_slices(h_ref.shape[0])
    xs = [h_ref[rows] for rows in slices]
    for g_pre, wg, wu, wd, g_post in ffn_refs:
        xs = _ffn_half_step(xs, g_pre[...], wg, wu, wd, g_post[...])
    for rows, x in zip(slices, xs):
        h_out_ref[rows] = x
        u = _rms(x, g_mix_ref[...]).astype(BF16)
        if mixer == "pool":
            proj = jnp.dot(u, w_in_ref[...], preferred_element_type=F32)
            tok_ref[rows] = proj[:, :D_TOK]
            qm_ref[rows] = proj[:, D_TOK:].astype(BF16)
        else:
            q = jnp.dot(u, w_q_ref[...], preferred_element_type=F32)
            q_ref[rows] = (q * (HEAD_DIM ** -0.5)).astype(BF16)
            kt = lax.dot_general(w_kt_ref[...], u, (((1,), (1,)), ((), ())),
                                 preferred_element_type=F32)
            for g in range(D_TOK // SB_LANES):
                for t in range(kt.shape[1] // SB_TILE):
                    kt_ref[g, rows.start // SB_TILE + t] = kt[
                        g * SB_LANES:(g + 1) * SB_LANES,
                        t * SB_TILE:(t + 1) * SB_TILE].astype(BF16)
            v_ref[rows] = jnp.dot(u, w_v_ref[...], preferred_element_type=F32).astype(BF16)
            qm_ref[rows] = jnp.dot(u, w_qm_ref[...], preferred_element_type=F32).astype(BF16)


def _head_call(h, params, layer, mixer, ffn_list):
    b, s, d = h.shape
    tm = min(TOKEN_TILE, s)
    d_ff = params["ffn1_gate"].shape[2]
    tile = lambda w: pl.BlockSpec((None, tm, w), lambda bi, j: (bi, j, 0))
    in_specs = [tile(d)]
    args = [h]
    for (li, which) in ffn_list:
        in_specs += _ffn_specs(d, d_ff, li, which)
        pre = "ffn1" if which == 1 else "ffn2"
        args += [params["g_pre"], params[pre + "_gate"], params[pre + "_up"],
                 params[pre + "_down"], params["g_post"]]
    in_specs.append(_const_spec((1, d), 3 * layer + 1))
    args.append(params["g_pre"])
    j = layer // 2
    if mixer == "pool":
        in_specs.append(_const_spec((d, D_TOK + D_MEMH), j))
        args.append(params["w_in_pool"])
        out_specs = [tile(d), tile(D_TOK), tile(D_MEMH)]
        out_shape = [jax.ShapeDtypeStruct((b, s, d), F32),
                     jax.ShapeDtypeStruct((b, s, D_TOK), F32),
                     jax.ShapeDtypeStruct((b, s, D_MEMH), BF16)]
    else:
        in_specs += [_const_spec((d, D_TOK), j), _const_spec((D_TOK, d), j),
                     _const_spec((d, D_TOK), j), _const_spec((d, D_MEMH), j)]
        args += [params["w_sb_q"], params["w_sb_kt"], params["w_sb_v"], params["w_sb_qm"]]
        n_group = D_TOK // SB_LANES
        out_specs = [tile(d), tile(D_TOK),
                     pl.BlockSpec((None, n_group, tm // SB_TILE, SB_LANES, SB_TILE),
                                  lambda bi, j: (bi, 0, j, 0, 0)),
                     tile(D_TOK), tile(D_MEMH)]
        out_shape = [jax.ShapeDtypeStruct((b, s, d), F32),
                     jax.ShapeDtypeStruct((b, s, D_TOK), BF16),
                     jax.ShapeDtypeStruct((b, n_group, s // SB_TILE, SB_LANES, SB_TILE), BF16),
                     jax.ShapeDtypeStruct((b, s, D_TOK), BF16),
                     jax.ShapeDtypeStruct((b, s, D_MEMH), BF16)]
    return pl.pallas_call(
        functools.partial(_head_kernel, mixer=mixer, n_ffn=len(ffn_list)),
        grid=(b, s // tm),
        in_specs=in_specs, out_specs=out_specs, out_shape=out_shape,
        compiler_params=pltpu.CompilerParams(
            dimension_semantics=("arbitrary", "arbitrary"), vmem_limit_bytes=VMEM_LIMIT),
        name="head_" + mixer,
    )(*args)


def _pool_mixer(u, ext_ref, pw_ref, ps, first_tile, tile_start):
    t = u.shape[0]

    @pl.when(first_tile)
    def _():
        ext_ref[0:POOL_HALO, :] = jnp.zeros((POOL_HALO, D_TOK), F32)

    @pl.when(jnp.logical_not(first_tile))
    def _():
        ext_ref[0:POOL_HALO, :] = ext_ref[t:t + POOL_HALO, :]

    ext_ref[POOL_HALO:POOL_HALO + t, :] = u
    pos = tile_start + lax.broadcasted_iota(jnp.int32, (t, POOL_GROUP), 0)
    outs = []
    for gi, w in enumerate(POOL_WINDOWS):
        cs = slice(gi * POOL_GROUP, (gi + 1) * POOL_GROUP)
        win = ext_ref[POOL_HALO:POOL_HALO + t, cs]
        for back in range(1, w):
            win = win + ext_ref[POOL_HALO - back:POOL_HALO - back + t, cs]
        cnt = jnp.minimum(pos + 1, w).astype(F32)
        dlt = (win / cnt - u[:, cs]).astype(BF16)
        outs.append(jnp.dot(dlt, pw_ref[gi], preferred_element_type=F32))
    return (jnp.concatenate(outs, axis=-1) * ps).astype(BF16)


def _tail_kernel(*refs, mixer, n_ffn):
    it = iter(refs)
    h_ref, tok_ref, qm_ref, kmt_ref, vm_ref = (next(it) for _ in range(5))
    if mixer == "pool":
        pw_ref, ps_ref = next(it), next(it)
    w_out_ref, g_post_ref = next(it), next(it)
    ffn_refs = [tuple(next(it) for _ in range(5)) for _ in range(n_ffn)]
    h_out_ref = next(it)
    if mixer == "pool":
        ext_ref = next(it)
        j = pl.program_id(1)
        tok = _pool_mixer(tok_ref[...], ext_ref, pw_ref, ps_ref[...], j == 0,
                          j * tok_ref.shape[0])
    slices = _row_slices(h_ref.shape[0])
    xs = []
    for rows in slices:
        mo = _mem_attention(qm_ref[rows], kmt_ref, vm_ref)
        cat = jnp.concatenate([tok[rows] if mixer == "pool" else tok_ref[rows], mo], axis=-1)
        mix = jnp.dot(cat, w_out_ref[...], preferred_element_type=F32)
        xs.append(h_ref[rows] + _rms(mix, g_post_ref[...]))
    for g_pre, wg, wu, wd, g_post in ffn_refs:
        xs = _ffn_half_step(xs, g_pre[...], wg, wu, wd, g_post[...])
    for rows, x in zip(slices, xs):
        h_out_ref[rows] = x


def _tail_call(h, tok, qm, kmt, vm, params, layer, mixer, ffn_list):
    b, s, d = h.shape
    tm = min(TOKEN_TILE, s)
    d_ff = params["ffn1_gate"].shape[2]
    l = kmt.shape[-1]
    tile = lambda w: pl.BlockSpec((None, tm, w), lambda bi, j: (bi, j, 0))
    in_specs = [tile(d), tile(D_TOK), tile(D_MEMH),
                pl.BlockSpec((None, None, D_MEMH, l), lambda bi, j: (layer, bi, 0, 0)),
                pl.BlockSpec((None, None, l, D_MEMH), lambda bi, j: (layer, bi, 0, 0))]
    args = [h, tok, qm, kmt, vm]
    scratch = []
    if mixer == "pool":
        jl = layer // 2
        in_specs += [_const_spec((len(POOL_WINDOWS), POOL_GROUP, POOL_GROUP), jl),
                     _const_spec((1, D_TOK), jl)]
        args += [params["pool_w"], params["pool_scale"]]
        scratch = [pltpu.VMEM((POOL_HALO + tm, D_TOK), F32)]
    in_specs += [_const_spec((D_TOK + D_MEMH, d), layer), _const_spec((1, d), 3 * layer + 1)]
    args += [params["w_out"], params["g_post"]]
    for (li, which) in ffn_list:
        in_specs += _ffn_specs(d, d_ff, li, which)
        pre = "ffn1" if which == 1 else "ffn2"
        args += [params["g_pre"], params[pre + "_gate"], params[pre + "_up"],
                 params[pre + "_down"], params["g_post"]]
    return pl.pallas_call(
        functools.partial(_tail_kernel, mixer=mixer, n_ffn=len(ffn_list)),
        grid=(b, s // tm),
        in_specs=in_specs, out_specs=tile(d),
        out_shape=jax.ShapeDtypeStruct((b, s, d), F32),
        scratch_shapes=scratch,
        compiler_params=pltpu.CompilerParams(
            dimension_semantics=("arbitrary", "arbitrary"), vmem_limit_bytes=VMEM_LIMIT),
        name="tail_" + mixer,
    )(*args)


def _first_key_tile(qi):
    return (qi + 1) * (SB_TQ // SB_TILE) - 1


def _next_tile(qi, kj, wrap):
    qn = qi + wrap.astype(jnp.int32)
    return qn, jnp.where(wrap, _first_key_tile(qn), kj - 1)


def _sb_kernel(q_ref, kt_ref, v_ref, o_ref, *scratch):
    z_refs, ls_refs, hl_refs, a_refs, top_refs = (scratch[2 * i:2 * i + 2] for i in range(5))
    acc_ref, later_ref, qexp_ref, vexp_ref = scratch[10:]
    tq, tk = SB_TQ, SB_TILE
    nk = kt_ref.shape[0]
    nq = nk * tk // tq
    n_stage = 4
    heads = range(SB_HEADS)

    row = lax.broadcasted_iota(jnp.int32, (tq, tk), 0)
    col = lax.broadcasted_iota(jnp.int32, (tq, tk), 1)
    ntri = jnp.where(lax.broadcasted_iota(jnp.int32, (tk, tk), 0)
                     >= lax.broadcasted_iota(jnp.int32, (tk, tk), 1),
                     -1.0, 0.0).astype(BF16)
    ntri2 = jnp.concatenate([ntri, ntri], axis=0)
    for ref in scratch[:12]:
        ref[...] = jnp.zeros(ref.shape, ref.dtype)

    lane = lax.broadcasted_iota(jnp.int32, (tk, SB_LANES), 1)

    def expand_values(t, k_sq):
        vt = v_ref[pl.ds(pl.multiple_of(t * tk, tk), tk), :]
        for h in heads:
            own = (lane >= h * HEAD_DIM) & (lane < (h + 1) * HEAD_DIM)
            vexp_ref[t, h * tk:(h + 1) * tk, :] = jnp.where(own, vt, jnp.zeros_like(vt))
        ksq = jnp.square(kt_ref[t].astype(F32))
        return tuple(jnp.maximum(m, jnp.sum(ksq[h * HEAD_DIM:(h + 1) * HEAD_DIM], axis=0,
                                            keepdims=True)) for h, m in zip(heads, k_sq))

    k_sq = lax.fori_loop(0, nk, expand_values, (jnp.zeros((1, tk), F32),) * SB_HEADS)

    def query_norms(t, q_sq):
        qsq = jnp.square(q_ref[pl.ds(pl.multiple_of(t * tk, tk), tk), :].astype(F32))
        return tuple(jnp.maximum(m, jnp.sum(
            jnp.where((lane >= h * HEAD_DIM) & (lane < (h + 1) * HEAD_DIM), qsq, 0.0),
            axis=1, keepdims=True)) for h, m in zip(heads, q_sq))

    q_sq = lax.fori_loop(0, nk, query_norms, (jnp.zeros((tk, 1), F32),) * SB_HEADS)
    z_bound = None
    for h in heads:
        prod = (jnp.max(q_sq[h], axis=0, keepdims=True)
                * jnp.max(k_sq[h], axis=1, keepdims=True))
        z_bound = prod if z_bound is None else jnp.maximum(z_bound, prod)
    z_bound = jnp.sqrt(z_bound) * (LOG2E * 1.01)

    def expand_queries(qi):
        qt = q_ref[pl.ds(pl.multiple_of(qi * tq, tq), tq), :]
        lane_q = lax.broadcasted_iota(jnp.int32, (tq, SB_LANES), 1)
        for h in heads:
            own = (lane_q >= h * HEAD_DIM) & (lane_q < (h + 1) * HEAD_DIM)
            qexp_ref[h * tq:(h + 1) * tq] = jnp.where(own, qt, jnp.zeros_like(qt))

    def step(cur, tiles):
        (q0, k0), _, (q2, _), (q3, k3) = tiles
        prev = 1 - cur
        z_ref, z_in = z_refs[cur], z_refs[prev]
        ls_ref, ls_in = ls_refs[cur], ls_refs[prev]
        hl_ref, hl_in = hl_refs[cur], hl_refs[prev]
        a_ref, a_in = a_refs[cur], a_refs[prev]
        top_ref, top_in = top_refs[cur], top_refs[prev]

        qc = jnp.minimum(q0, nq - 1)
        kc = jnp.minimum(k0, nk - 1)

        dead = jnp.min(jnp.where(top_in[...] + z_bound < -SB_SKIP, 1, 0)) == 1
        wrap = (k0 == 0) | (dead & (q0 == q2))

        def stage_p():
            pv = jnp.dot(a_in[...], vexp_ref[k3], preferred_element_type=F32)
            acc = jnp.where(k3 == _first_key_tile(q3), pv, acc_ref[...] + pv)
            acc_ref[...] = acc
            o_ref[pl.ds(pl.multiple_of(q3 * tq, tq), tq), :] = acc.astype(BF16)

        def stage_l(rows):
            z = z_in[rows]
            nlf = jnp.where(z > SB_BIG, z, jnp.log2(1.0 + jnp.exp2(z)))
            later = later_ref[rows]
            ls_ref[rows] = z + later
            hi = nlf.astype(BF16)
            hl_ref[rows, 0:tk] = hi
            hl_ref[rows, tk:2 * tk] = (nlf - hi.astype(F32)).astype(BF16)
            later = later - jnp.sum(nlf, axis=-1, keepdims=True)
            later_ref[rows] = later
            return later

        stage_p()
        tops = []
        for g in range(0, SB_HEADS, SB_DOT_HEADS):
            grp = slice(g * tq, (g + SB_DOT_HEADS) * tq)
            zz = jnp.dot(qexp_ref[grp], kt_ref[kc], preferred_element_type=F32)
            ss = jnp.dot(hl_in[grp], ntri2, preferred_element_type=F32)
            for r in range(0, SB_DOT_HEADS * tq, SB_ROWS):
                rows = slice(g * tq + r, g * tq + r + SB_ROWS)
                tops.append(stage_l(rows))
                h, hr = g + r // tq, r % tq
                a_ref[hr:hr + SB_ROWS, h * tk:(h + 1) * tk] = jnp.exp2(
                    ls_in[rows] + ss[r:r + SB_ROWS]).astype(BF16)
                z_ref[rows] = zz[r:r + SB_ROWS] * LOG2E
        while len(tops) > 1:
            tops = [jnp.maximum(a, b) for a, b in zip(tops[0::2], tops[1::2])]
        top_ref[...] = tops[0]

        @pl.when(kc * tk + tk > qc * tq)
        def _():
            bias = jnp.where(kc * tk + col < qc * tq + row, 0.0, MASK_BIAS)
            for h in heads:
                z_ref[h * tq:(h + 1) * tq] += bias

        @pl.when(kc == _first_key_tile(qc))
        def _():
            later_ref[...] = jnp.zeros(later_ref.shape, F32)

        @pl.when(wrap)
        def _():
            expand_queries(jnp.minimum(q0 + 1, nq - 1))

        return (_next_tile(q0, k0, wrap),) + tiles[:-1]

    expand_queries(0)
    start = (jnp.int32(0), jnp.int32(_first_key_tile(0)))
    tiles = lax.while_loop(lambda tiles: tiles[-2][0] < nq,
                           lambda tiles: step(1, step(0, tiles)), (start,) * n_stage)

    @pl.when(tiles[-1][0] < nq)
    def _():
        step(0, tiles)


def _sb_call(q, kt, v):
    b, s, _ = q.shape
    n_group = D_TOK // SB_LANES
    nk = s // SB_TILE
    tq = SB_TQ
    assert s % tq == 0 and tq % SB_TILE == 0
    rows_f32 = pltpu.VMEM((SB_HEADS * tq, SB_TILE), F32)
    return pl.pallas_call(
        _sb_kernel,
        grid=(b, n_group),
        in_specs=[
            pl.BlockSpec((None, s, SB_LANES), lambda bi, g: (bi, 0, g),
                         pipeline_mode=pl.Buffered(1)),
            pl.BlockSpec((None, None, nk, SB_LANES, SB_TILE), lambda bi, g: (bi, g, 0, 0, 0),
                         pipeline_mode=pl.Buffered(1)),
            pl.BlockSpec((None, s, SB_LANES), lambda bi, g: (bi, 0, g),
                         pipeline_mode=pl.Buffered(1)),
        ],
        out_specs=pl.BlockSpec((None, s, SB_LANES), lambda bi, g: (bi, 0, g)),
        out_shape=jax.ShapeDtypeStruct((b, s, D_TOK), BF16),
        scratch_shapes=2 * [rows_f32]
                       + 2 * [rows_f32]
                       + 2 * [pltpu.VMEM((SB_HEADS * tq, 2 * SB_TILE), BF16)]
                       + 2 * [pltpu.VMEM((tq, SB_HEADS * SB_TILE), BF16)]
                       + 2 * [pltpu.VMEM((SB_ROWS, 1), F32)]
                       + [pltpu.VMEM((tq, SB_LANES), F32),
                          pltpu.VMEM((SB_HEADS * tq, 1), F32),
                          pltpu.VMEM((SB_HEADS * tq, SB_LANES), BF16),
                          pltpu.VMEM((nk, SB_HEADS * SB_TILE, SB_LANES), BF16)],
        compiler_params=pltpu.CompilerParams(
            dimension_semantics=("arbitrary", "arbitrary"), vmem_limit_bytes=VMEM_LIMIT),
        name="stick_breaking",
    )(q, kt, v)


def kernel(x, mem, g_pre, g_post, g_mem, ffn1_gate, ffn1_up, ffn1_down, ffn2_gate, ffn2_up,
           ffn2_down, w_in_pool, pool_w, pool_scale, w_in_sb, w_mem_kv, w_out):
    depth, _, d = g_pre.shape
    assert x.shape[1] % max(SB_TILE, TOKEN_TILE) == 0 or x.shape[1] == SB_TILE
    params = {
        "g_pre": g_pre.reshape(depth * 3, 1, d),
        "g_post": g_post.reshape(depth * 3, 1, d),
        "ffn1_gate": ffn1_gate.astype(BF16), "ffn1_up": ffn1_up.astype(BF16),
        "ffn1_down": ffn1_down.astype(BF16),
        "ffn2_gate": ffn2_gate.astype(BF16), "ffn2_up": ffn2_up.astype(BF16),
        "ffn2_down": ffn2_down.astype(BF16),
        "w_in_pool": w_in_pool.astype(BF16),
        "pool_w": pool_w.astype(BF16),
        "pool_scale": pool_scale.reshape(pool_scale.shape[0], 1, D_TOK),
        "w_sb_q": w_in_sb[:, :, :D_TOK].astype(BF16),
        "w_sb_kt": jnp.swapaxes(w_in_sb[:, :, D_TOK:2 * D_TOK], 1, 2).astype(BF16),
        "w_sb_v": w_in_sb[:, :, 2 * D_TOK:3 * D_TOK].astype(BF16),
        "w_sb_qm": w_in_sb[:, :, 3 * D_TOK:].astype(BF16),
        "w_out": w_out.astype(BF16),
    }
    kmt, vm = _memkv_call(mem, g_mem,
                          jnp.swapaxes(w_mem_kv[:, :, :D_MEMH], 1, 2).astype(BF16),
                          w_mem_kv[:, :, D_MEMH:].astype(BF16))
    h = x
    for i in range(depth):
        mixer = "pool" if i % 2 == 0 else "sb"
        if mixer == "pool":
            h, tok, qm = _head_call(h, params, i, mixer, [(i, 1)])
        else:
            h, q, kt, v, qm = _head_call(h, params, i, mixer, [(i, 1)])
            tok = _sb_call(q, kt, v)
        h = _tail_call(h, tok, qm, kmt, vm, params, i, mixer, [(i, 2)])
    return h
```

```python
import functools

import jax
import jax.numpy as jnp
from jax import lax
from jax.experimental import pallas as pl
from jax.experimental.pallas import tpu as pltpu

F32 = jnp.float32
BF16 = jnp.bfloat16

EPS = 1e-6
HEAD_DIM = 64
SB_HEADS = 4
SB_LANES = SB_HEADS * HEAD_DIM
MASK_BIAS = -1e30
SB_ROWS = 32
SB_DOT_HEADS = 2
SB_BIG = 64.0
SB_SKIP = 160.0
LOG2E = 1.4426950408889634
POOL_WINDOWS = (2, 4, 8, 16)
POOL_GROUP = 128
POOL_HALO = 16
D_TOK = 512
D_MEMH = 256
N_MEM_HEADS = D_MEMH // HEAD_DIM
SB_TILE = 256
SB_TQ = 256
FF_CHUNK = 512
ROW_SLICE = 512
TOKEN_TILE = 1024
VMEM_LIMIT = 56 * 1024 * 1024


def _rms(x, g):
    ms = jnp.mean(x * x, axis=-1, keepdims=True)
    return x * lax.rsqrt(ms + EPS) * g


def _row_slices(t):
    size = min(t, ROW_SLICE)
    return [slice(r, r + size) for r in range(0, t, size)]


def _ffn_half_step(xs, g_pre, wg_ref, wu_ref, wd_ref, g_post):
    xns = [_rms(x, g_pre).astype(BF16) for x in xs]
    d_ff = wg_ref.shape[1]
    fs = [None] * len(xs)
    for c in range(d_ff // FF_CHUNK):
        cs = slice(c * FF_CHUNK, (c + 1) * FF_CHUNK)
        for i, xn in enumerate(xns):
            gate = jnp.dot(xn, wg_ref[:, cs], preferred_element_type=F32)
            up = jnp.dot(xn, wu_ref[:, cs], preferred_element_type=F32)
            mid = (gate * (1.0 / (1.0 + jnp.exp(-gate))) * up).astype(BF16)
            part = jnp.dot(mid, wd_ref[cs, :], preferred_element_type=F32)
            fs[i] = part if fs[i] is None else fs[i] + part
    return [x + 0.5 * _rms(f, g_post) for x, f in zip(xs, fs)]


def _mem_attention(qm, kmt_ref, vm_ref):
    t = qm.shape[0]
    lane = lax.broadcasted_iota(jnp.int32, (t, D_MEMH), 1)
    kmt = kmt_ref[...]
    vm = vm_ref[...]
    out = jnp.zeros((t, D_MEMH), F32)
    for h in range(N_MEM_HEADS):
        sel = (lane >= h * HEAD_DIM) & (lane < (h + 1) * HEAD_DIM)
        qh = jnp.where(sel, qm, jnp.zeros_like(qm))
        s = jnp.dot(qh, kmt, preferred_element_type=F32) * (HEAD_DIM ** -0.5)
        e = jnp.exp(s - jnp.max(s, axis=-1, keepdims=True))
        p = (e / jnp.sum(e, axis=-1, keepdims=True)).astype(BF16)
        o = jnp.dot(p, vm, preferred_element_type=F32)
        out = jnp.where(sel, o, out)
    return out.astype(BF16)


def _memkv_kernel(mem_ref, g_ref, wkt_ref, wv_ref, kmt_ref, vm_ref):
    mn = _rms(mem_ref[...], g_ref[...]).astype(BF16)
    kmt = lax.dot_general(wkt_ref[...], mn, (((1,), (1,)), ((), ())),
                          preferred_element_type=F32)
    kmt_ref[...] = kmt.astype(BF16)
    vm_ref[...] = jnp.dot(mn, wv_ref[...], preferred_element_type=F32).astype(BF16)


def _memkv_call(mem, g_mem, w_kt, w_v):
    b, l, d = mem.shape
    depth = g_mem.shape[0]
    return pl.pallas_call(
        _memkv_kernel,
        grid=(b, depth),
        in_specs=[
            pl.BlockSpec((None, l, d), lambda bi, i: (bi, 0, 0)),
            pl.BlockSpec((None, 1, d), lambda bi, i: (i, 0, 0)),
            pl.BlockSpec((None, D_MEMH, d), lambda bi, i: (i, 0, 0)),
            pl.BlockSpec((None, d, D_MEMH), lambda bi, i: (i, 0, 0)),
        ],
        out_specs=[
            pl.BlockSpec((None, None, D_MEMH, l), lambda bi, i: (i, bi, 0, 0)),
            pl.BlockSpec((None, None, l, D_MEMH), lambda bi, i: (i, bi, 0, 0)),
        ],
        out_shape=[
            jax.ShapeDtypeStruct((depth, b, D_MEMH, l), BF16),
            jax.ShapeDtypeStruct((depth, b, l, D_MEMH), BF16),
        ],
        compiler_params=pltpu.CompilerParams(
            dimension_semantics=("arbitrary", "arbitrary"), vmem_limit_bytes=VMEM_LIMIT),
        name="memkv",
    )(mem, g_mem.reshape(depth, 1, d), w_kt, w_v)


def _const_spec(shape, layer):
    nd = len(shape)
    return pl.BlockSpec((None,) + tuple(shape), lambda bi, j: (layer,) + (0,) * nd,
                        pipeline_mode=pl.Buffered(1))


def _ffn_specs(d, d_ff, layer, which):
    g_idx = 3 * layer + (0 if which == 1 else 2)
    return [
        _const_spec((1, d), g_idx),
        _const_spec((d, d_ff), layer),
        _const_spec((d, d_ff), layer),
        _const_spec((d_ff, d), layer),
        _const_spec((1, d), g_idx),
    ]


def _head_kernel(*refs, mixer, n_ffn):
    it = iter(refs)
    h_ref = next(it)
    ffn_refs = [tuple(next(it) for _ in range(5)) for _ in range(n_ffn)]
    g_mix_ref = next(it)
    if mixer == "pool":
        w_in_ref = next(it)
        h_out_ref, tok_ref, qm_ref = it
    else:
        w_q_ref, w_kt_ref, w_v_ref, w_qm_ref = (next(it) for _ in range(4))
        h_out_ref, q_ref, kt_ref, v_ref, qm_ref = it
    slices = _row_slices(h_ref.shape[0])
    xs = [h_ref[rows] for rows in slices]
    for g_pre, wg, wu, wd, g_post in ffn_refs:
        xs = _ffn_half_step(xs, g_pre[...], wg, wu, wd, g_post[...])
    for rows, x in zip(slices, xs):
        h_out_ref[rows] = x
        u = _rms(x, g_mix_ref[...]).astype(BF16)
        if mixer == "pool":
            proj = jnp.dot(u, w_in_ref[...], preferred_element_type=F32)
            tok_ref[rows] = proj[:, :D_TOK]
            qm_ref[rows] = proj[:, D_TOK:].astype(BF16)
        else:
            q = jnp.dot(u, w_q_ref[...], preferred_element_type=F32)
            q_ref[rows] = (q * (HEAD_DIM ** -0.5)).astype(BF16)
            kt = lax.dot_general(w_kt_ref[...], u, (((1,), (1,)), ((), ())),
                                 preferred_element_type=F32)
            for g in range(D_TOK // SB_LANES):
                for t in range(kt.shape[1] // SB_TILE):
                    kt_ref[g, rows.start // SB_TILE + t] = kt[
                        g * SB_LANES:(g + 1) * SB_LANES,
                        t * SB_TILE:(t + 1) * SB_TILE].astype(BF16)
            v_ref[rows] = jnp.dot(u, w_v_ref[...], preferred_element_type=F32).astype(BF16)
            qm_ref[rows] = jnp.dot(u, w_qm_ref[...], preferred_element_type=F32).astype(BF16)


def _head_call(h, params, layer, mixer, ffn_list):
    b, s, d = h.shape
    tm = min(TOKEN_TILE, s)
    d_ff = params["ffn1_gate"].shape[2]
    tile = lambda w: pl.BlockSpec((None, tm, w), lambda bi, j: (bi, j, 0))
    in_specs = [tile(d)]
    args = [h]
    for (li, which) in ffn_list:
        in_specs += _ffn_specs(d, d_ff, li, which)
        pre = "ffn1" if which == 1 else "ffn2"
        args += [params["g_pre"], params[pre + "_gate"], params[pre + "_up"],
                 params[pre + "_down"], params["g_post"]]
    in_specs.append(_const_spec((1, d), 3 * layer + 1))
    args.append(params["g_pre"])
    j = layer // 2
    if mixer == "pool":
        in_specs.append(_const_spec((d, D_TOK + D_MEMH), j))
        args.append(params["w_in_pool"])
        out_specs = [tile(d), tile(D_TOK), tile(D_MEMH)]
        out_shape = [jax.ShapeDtypeStruct((b, s, d), F32),
                     jax.ShapeDtypeStruct((b, s, D_TOK), F32),
                     jax.ShapeDtypeStruct((b, s, D_MEMH), BF16)]
    else:
        in_specs += [_const_spec((d, D_TOK), j), _const_spec((D_TOK, d), j),
                     _const_spec((d, D_TOK), j), _const_spec((d, D_MEMH), j)]
        args += [params["w_sb_q"], params["w_sb_kt"], params["w_sb_v"], params["w_sb_qm"]]
        n_group = D_TOK // SB_LANES
        out_specs = [tile(d), tile(D_TOK),
                     pl.BlockSpec((None, n_group, tm // SB_TILE, SB_LANES, SB_TILE),
                                  lambda bi, j: (bi, 0, j, 0, 0)),
                     tile(D_TOK), tile(D_MEMH)]
        out_shape = [jax.ShapeDtypeStruct((b, s, d), F32),
                     jax.ShapeDtypeStruct((b, s, D_TOK), BF16),
                     jax.ShapeDtypeStruct((b, n_group, s // SB_TILE, SB_LANES, SB_TILE), BF16),
                     jax.ShapeDtypeStruct((b, s, D_TOK), BF16),
                     jax.ShapeDtypeStruct((b, s, D_MEMH), BF16)]
    return pl.pallas_call(
        functools.partial(_head_kernel, mixer=mixer, n_ffn=len(ffn_list)),
        grid=(b, s // tm),
        in_specs=in_specs, out_specs=out_specs, out_shape=out_shape,
        compiler_params=pltpu.CompilerParams(
            dimension_semantics=("arbitrary", "arbitrary"), vmem_limit_bytes=VMEM_LIMIT),
        name="head_" + mixer,
    )(*args)


def _pool_mixer(u, ext_ref, pw_ref, ps, first_tile, tile_start):
    t = u.shape[0]

    @pl.when(first_tile)
    def _():
        ext_ref[0:POOL_HALO, :] = jnp.zeros((POOL_HALO, D_TOK), F32)

    @pl.when(jnp.logical_not(first_tile))
    def _():
        ext_ref[0:POOL_HALO, :] = ext_ref[t:t + POOL_HALO, :]

    ext_ref[POOL_HALO:POOL_HALO + t, :] = u
    pos = tile_start + lax.broadcasted_iota(jnp.int32, (t, POOL_GROUP), 0)
    outs = []
    for gi, w in enumerate(POOL_WINDOWS):
        cs = slice(gi * POOL_GROUP, (gi + 1) * POOL_GROUP)
        win = ext_ref[POOL_HALO:POOL_HALO + t, cs]
        for back in range(1, w):
            win = win + ext_ref[POOL_HALO - back:POOL_HALO - back + t, cs]
        cnt = jnp.minimum(pos + 1, w).astype(F32)
        dlt = (win / cnt - u[:, cs]).astype(BF16)
        outs.append(jnp.dot(dlt, pw_ref[gi], preferred_element_type=F32))
    return (jnp.concatenate(outs, axis=-1) * ps).astype(BF16)


def _tail_kernel(*refs, mixer, n_ffn):
    it = iter(refs)
    h_ref, tok_ref, qm_ref, kmt_ref, vm_ref = (next(it) for _ in range(5))
    if mixer == "pool":
        pw_ref, ps_ref = next(it), next(it)
    w_out_ref, g_post_ref = next(it), next(it)
    ffn_refs = [tuple(next(it) for _ in range(5)) for _ in range(n_ffn)]
    h_out_ref = next(it)
    if mixer == "pool":
        ext_ref = next(it)
        j = pl.program_id(1)
        tok = _pool_mixer(tok_ref[...], ext_ref, pw_ref, ps_ref[...], j == 0,
                          j * tok_ref.shape[0])
    slices = _row_slices(h_ref.shape[0])
    xs = []
    for rows in slices:
        mo = _mem_attention(qm_ref[rows], kmt_ref, vm_ref)
        cat = jnp.concatenate([tok[rows] if mixer == "pool" else tok_ref[rows], mo], axis=-1)
        mix = jnp.dot(cat, w_out_ref[...], preferred_element_type=F32)
        xs.append(h_ref[rows] + _rms(mix, g_post_ref[...]))
    for g_pre, wg, wu, wd, g_post in ffn_refs:
        xs = _ffn_half_step(xs, g_pre[...], wg, wu, wd, g_post[...])
    for rows, x in zip(slices, xs):
        h_out_ref[rows] = x


def _tail_call(h, tok, qm, kmt, vm, params, layer, mixer, ffn_list):
    b, s, d = h.shape
    tm = min(TOKEN_TILE, s)
    d_ff = params["ffn1_gate"].shape[2]
    l = kmt.shape[-1]
    tile = lambda w: pl.BlockSpec((None, tm, w), lambda bi, j: (bi, j, 0))
    in_specs = [tile(d), tile(D_TOK), tile(D_MEMH),
                pl.BlockSpec((None, None, D_MEMH, l), lambda bi, j: (layer, bi, 0, 0)),
                pl.BlockSpec((None, None, l, D_MEMH), lambda bi, j: (layer, bi, 0, 0))]
    args = [h, tok, qm, kmt, vm]
    scratch = []
    if mixer == "pool":
        jl = layer // 2
        in_specs += [_const_spec((len(POOL_WINDOWS), POOL_GROUP, POOL_GROUP), jl),
                     _const_spec((1, D_TOK), jl)]
        args += [params["pool_w"], params["pool_scale"]]
        scratch = [pltpu.VMEM((POOL_HALO + tm, D_TOK), F32)]
    in_specs += [_const_spec((D_TOK + D_MEMH, d), layer), _const_spec((1, d), 3 * layer + 1)]
    args += [params["w_out"], params["g_post"]]
    for (li, which) in ffn_list:
        in_specs += _ffn_specs(d, d_ff, li, which)
        pre = "ffn1" if which == 1 else "ffn2"
        args += [params["g_pre"], params[pre + "_gate"], params[pre + "_up"],
                 params[pre + "_down"], params["g_post"]]
    return pl.pallas_call(
        functools.partial(_tail_kernel, mixer=mixer, n_ffn=len(ffn_list)),
        grid=(b, s // tm),
        in_specs=in_specs, out_specs=tile(d),
        out_shape=jax.ShapeDtypeStruct((b, s, d), F32),
        scratch_shapes=scratch,
        compiler_params=pltpu.CompilerParams(
            dimension_semantics=("arbitrary", "arbitrary"), vmem_limit_bytes=VMEM_LIMIT),
        name="tail_" + mixer,
    )(*args)


def _first_key_tile(qi):
    return (qi + 1) * (SB_TQ // SB_TILE) - 1


def _next_tile(qi, kj, wrap):
    qn = qi + wrap.astype(jnp.int32)
    return qn, jnp.where(wrap, _first_key_tile(qn), kj - 1)


def _sb_kernel(q_ref, kt_ref, v_ref, o_ref, *scratch):
    z_refs, ls_refs, hl_refs, a_refs = (scratch[2 * i:2 * i + 2] for i in range(4))
    acc_ref, later_ref, qexp_ref, vexp_ref = scratch[8:]
    tq, tk = SB_TQ, SB_TILE
    nk = kt_ref.shape[0]
    nq = nk * tk // tq
    n_stage = 4
    heads = range(SB_HEADS)

    row = lax.broadcasted_iota(jnp.int32, (tq, tk), 0)
    col = lax.broadcasted_iota(jnp.int32, (tq, tk), 1)
    ntri = jnp.where(lax.broadcasted_iota(jnp.int32, (tk, tk), 0)
                     >= lax.broadcasted_iota(jnp.int32, (tk, tk), 1),
                     -1.0, 0.0).astype(BF16)
    ntri2 = jnp.concatenate([ntri, ntri], axis=0)
    for ref in scratch[:10]:
        ref[...] = jnp.zeros(ref.shape, ref.dtype)
    for ref in z_refs:
        ref[...] = jnp.full(ref.shape, MASK_BIAS, F32)

    lane = lax.broadcasted_iota(jnp.int32, (tk, SB_LANES), 1)

    def expand_values(t, norms):
        k_sq, q_sq = norms
        vt = v_ref[pl.ds(pl.multiple_of(t * tk, tk), tk), :]
        for h in heads:
            own = (lane >= h * HEAD_DIM) & (lane < (h + 1) * HEAD_DIM)
            vexp_ref[t, h * tk:(h + 1) * tk, :] = jnp.where(own, vt, jnp.zeros_like(vt))
        ksq = jnp.sum(jnp.square(kt_ref[t].astype(F32)), axis=0, keepdims=True)
        qsq = jnp.sum(jnp.square(
            q_ref[pl.ds(pl.multiple_of(t * tk, tk), tk), :].astype(F32)), axis=1, keepdims=True)
        return jnp.maximum(k_sq, ksq), jnp.maximum(q_sq, qsq)

    k_sq, q_sq = lax.fori_loop(0, nk, expand_values,
                               (jnp.zeros((1, tk), F32), jnp.zeros((tk, 1), F32)))
    z_bound = jnp.sqrt(jnp.max(q_sq, axis=0, keepdims=True)
                       * jnp.max(k_sq, axis=1, keepdims=True)) * (LOG2E * 1.01)

    def expand_queries(qi):
        qt = q_ref[pl.ds(pl.multiple_of(qi * tq, tq), tq), :]
        lane_q = lax.broadcasted_iota(jnp.int32, (tq, SB_LANES), 1)
        for h in heads:
            own = (lane_q >= h * HEAD_DIM) & (lane_q < (h + 1) * HEAD_DIM)
            qexp_ref[h * tq:(h + 1) * tq] = jnp.where(own, qt, jnp.zeros_like(qt))

    def step(cur, tiles):
        (q0, k0), (q1, _), _, (q3, k3) = tiles
        prev = 1 - cur
        z_ref, z_in = z_refs[cur], z_refs[prev]
        ls_ref, ls_in = ls_refs[cur], ls_refs[prev]
        hl_ref, hl_in = hl_refs[cur], hl_refs[prev]
        a_ref, a_in = a_refs[cur], a_refs[prev]

        qc = jnp.minimum(q0, nq - 1)
        kc = jnp.minimum(k0, nk - 1)

        def stage_p():
            pv = jnp.dot(a_in[...], vexp_ref[k3], preferred_element_type=F32)
            acc = jnp.where(k3 == _first_key_tile(q3), pv, acc_ref[...] + pv)
            acc_ref[...] = acc
            o_ref[pl.ds(pl.multiple_of(q3 * tq, tq), tq), :] = acc.astype(BF16)

        def stage_l(rows):
            z = z_in[rows]
            nlf = jnp.where(z > SB_BIG, z, jnp.log2(1.0 + jnp.exp2(z)))
            later = later_ref[rows]
            ls_ref[rows] = z + later
            hi = nlf.astype(BF16)
            hl_ref[rows, 0:tk] = hi
            hl_ref[rows, tk:2 * tk] = (nlf - hi.astype(F32)).astype(BF16)
            later = later - jnp.sum(nlf, axis=-1, keepdims=True)
            later_ref[rows] = later
            return later

        stage_p()
        tops = []
        for g in range(0, SB_HEADS, SB_DOT_HEADS):
            grp = slice(g * tq, (g + SB_DOT_HEADS) * tq)
            zz = jnp.dot(qexp_ref[grp], kt_ref[kc], preferred_element_type=F32)
            ss = jnp.dot(hl_in[grp], ntri2, preferred_element_type=F32)
            for r in range(0, SB_DOT_HEADS * tq, SB_ROWS):
                rows = slice(g * tq + r, g * tq + r + SB_ROWS)
                tops.append(stage_l(rows))
                h, hr = g + r // tq, r % tq
                a_ref[hr:hr + SB_ROWS, h * tk:(h + 1) * tk] = jnp.exp2(
                    ls_in[rows] + ss[r:r + SB_ROWS]).astype(BF16)
                z_ref[rows] = zz[r:r + SB_ROWS] * LOG2E
        while len(tops) > 1:
            tops = [jnp.maximum(a, b) for a, b in zip(tops[0::2], tops[1::2])]
        dead = jnp.min(jnp.where(tops[0] + z_bound < -SB_SKIP, 1, 0)) == 1
        wrap = (k0 == 0) | (dead & (q0 == q1))

        @pl.when(kc * tk + tk > qc * tq)
        def _():
            bias = jnp.where(kc * tk + col < qc * tq + row, 0.0, MASK_BIAS)
            for h in heads:
                z_ref[h * tq:(h + 1) * tq] += bias

        @pl.when(kc == _first_key_tile(qc))
        def _():
            later_ref[...] = jnp.zeros(later_ref.shape, F32)

        @pl.when(wrap)
        def _():
            expand_queries(jnp.minimum(q0 + 1, nq - 1))

        return (_next_tile(q0, k0, wrap),) + tiles[:-1]

    expand_queries(0)
    start = (jnp.int32(0), jnp.int32(_first_key_tile(0)))
    tiles = lax.while_loop(lambda tiles: tiles[-2][0] < nq,
                           lambda tiles: step(1, step(0, tiles)), (start,) * n_stage)

    @pl.when(tiles[-1][0] < nq)
    def _():
        step(0, tiles)


def _sb_call(q, kt, v):
    b, s, _ = q.shape
    n_group = D_TOK // SB_LANES
    nk = s // SB_TILE
    tq = SB_TQ
    assert s % tq == 0 and tq % SB_TILE == 0
    rows_f32 = pltpu.VMEM((SB_HEADS * tq, SB_TILE), F32)
    return pl.pallas_call(
        _sb_kernel,
        grid=(b, n_group),
        in_specs=[
            pl.BlockSpec((None, s, SB_LANES), lambda bi, g: (bi, 0, g),
                         pipeline_mode=pl.Buffered(1)),
            pl.BlockSpec((None, None, nk, SB_LANES, SB_TILE), lambda bi, g: (bi, g, 0, 0, 0),
                         pipeline_mode=pl.Buffered(1)),
            pl.BlockSpec((None, s, SB_LANES), lambda bi, g: (bi, 0, g),
                         pipeline_mode=pl.Buffered(1)),
        ],
        out_specs=pl.BlockSpec((None, s, SB_LANES), lambda bi, g: (bi, 0, g)),
        out_shape=jax.ShapeDtypeStruct((b, s, D_TOK), BF16),
        scratch_shapes=2 * [rows_f32]
                       + 2 * [rows_f32]
                       + 2 * [pltpu.VMEM((SB_HEADS * tq, 2 * SB_TILE), BF16)]
                       + 2 * [pltpu.VMEM((tq, SB_HEADS * SB_TILE), BF16)]
                       + [pltpu.VMEM((tq, SB_LANES), F32),
                          pltpu.VMEM((SB_HEADS * tq, 1), F32),
                          pltpu.VMEM((SB_HEADS * tq, SB_LANES), BF16),
                          pltpu.VMEM((nk, SB_HEADS * SB_TILE, SB_LANES), BF16)],
        compiler_params=pltpu.CompilerParams(
            dimension_semantics=("arbitrary", "arbitrary"), vmem_limit_bytes=VMEM_LIMIT),
        name="stick_breaking",
    )(q, kt, v)


def kernel(x, mem, g_pre, g_post, g_mem, ffn1_gate, ffn1_up, ffn1_down, ffn2_gate, ffn2_up,
           ffn2_down, w_in_pool, pool_w, pool_scale, w_in_sb, w_mem_kv, w_out):
    depth, _, d = g_pre.shape
    assert x.shape[1] % max(SB_TILE, TOKEN_TILE) == 0 or x.shape[1] == SB_TILE
    params = {
        "g_pre": g_pre.reshape(depth * 3, 1, d),
        "g_post": g_post.reshape(depth * 3, 1, d),
        "ffn1_gate": ffn1_gate.astype(BF16), "ffn1_up": ffn1_up.astype(BF16),
        "ffn1_down": ffn1_down.astype(BF16),
        "ffn2_gate": ffn2_gate.astype(BF16), "ffn2_up": ffn2_up.astype(BF16),
        "ffn2_down": ffn2_down.astype(BF16),
        "w_in_pool": w_in_pool.astype(BF16),
        "pool_w": pool_w.astype(BF16),
        "pool_scale": pool_scale.reshape(pool_scale.shape[0], 1, D_TOK),
        "w_sb_q": w_in_sb[:, :, :D_TOK].astype(BF16),
        "w_sb_kt": jnp.swapaxes(w_in_sb[:, :, D_TOK:2 * D_TOK], 1, 2).astype(BF16),
        "w_sb_v": w_in_sb[:, :, 2 * D_TOK:3 * D_TOK].astype(BF16),
        "w_sb_qm": w_in_sb[:, :, 3 * D_TOK:].astype(BF16),
        "w_out": w_out.astype(BF16),
    }
    kmt, vm = _memkv_call(mem, g_mem,
                          jnp.swapaxes(w_mem_kv[:, :, :D_MEMH], 1, 2).astype(BF16),
                          w_mem_kv[:, :, D_MEMH:].astype(BF16))
    h = x
    for i in range(depth):
        mixer = "pool" if i % 2 == 0 else "sb"
        if mixer == "pool":
            h, tok, qm = _head_call(h, params, i, mixer, [(i, 1)])
        else:
            h, q, kt, v, qm = _head_call(h, params, i, mixer, [(i, 1)])
            tok = _sb_call(q, kt, v)
        h = _tail_call(h, tok, qm, kmt, vm, params, i, mixer, [(i, 2)])
    return h
```

```python
import functools

import jax
import jax.numpy as jnp
from jax import lax
from jax.experimental import pallas as pl
from jax.experimental.pallas import tpu as pltpu

F32 = jnp.float32
BF16 = jnp.bfloat16

EPS = 1e-6
HEAD_DIM = 64
SB_HEADS = 4
SB_LANES = SB_HEADS * HEAD_DIM
MASK_BIAS = -1e30
SB_ROWS = 32
SB_DOT_HEADS = 2
SB_BIG = 64.0
SB_SKIP = 160.0
LOG2E = 1.4426950408889634
POOL_WINDOWS = (2, 4, 8, 16)
POOL_GROUP = 128
POOL_HALO = 16
D_TOK = 512
D_MEMH = 256
N_MEM_HEADS = D_MEMH // HEAD_DIM
SB_TILE = 256
SB_TQ = 256
FF_CHUNK = 512
ROW_SLICE = 512
TOKEN_TILE = 1024
VMEM_LIMIT = 56 * 1024 * 1024


def _rms(x, g):
    ms = jnp.mean(x * x, axis=-1, keepdims=True)
    return x * lax.rsqrt(ms + EPS) * g


def _row_slices(t):
    size = min(t, ROW_SLICE)
    return [slice(r, r + size) for r in range(0, t, size)]


def _ffn_half_step(xs, g_pre, wg_ref, wu_ref, wd_ref, g_post):
    xns = [_rms(x, g_pre).astype(BF16) for x in xs]
    d_ff = wg_ref.shape[1]
    fs = [None] * len(xs)
    for c in range(d_ff // FF_CHUNK):
        cs = slice(c * FF_CHUNK, (c + 1) * FF_CHUNK)
        for i, xn in enumerate(xns):
            gate = jnp.dot(xn, wg_ref[:, cs], preferred_element_type=F32)
            up = jnp.dot(xn, wu_ref[:, cs], preferred_element_type=F32)
            mid = (gate * (1.0 / (1.0 + jnp.exp(-gate))) * up).astype(BF16)
            part = jnp.dot(mid, wd_ref[cs, :], preferred_element_type=F32)
            fs[i] = part if fs[i] is None else fs[i] + part
    return [x + 0.5 * _rms(f, g_post) for x, f in zip(xs, fs)]


def _mem_attention(qm, kmt_ref, vm_ref):
    t = qm.shape[0]
    lane = lax.broadcasted_iota(jnp.int32, (t, D_MEMH), 1)
    kmt = kmt_ref[...]
    vm = vm_ref[...]
    out = jnp.zeros((t, D_MEMH), F32)
    for h in range(N_MEM_HEADS):
        sel = (lane >= h * HEAD_DIM) & (lane < (h + 1) * HEAD_DIM)
        qh = jnp.where(sel, qm, jnp.zeros_like(qm))
        s = jnp.dot(qh, kmt, preferred_element_type=F32) * (HEAD_DIM ** -0.5)
        e = jnp.exp(s - jnp.max(s, axis=-1, keepdims=True))
        p = (e / jnp.sum(e, axis=-1, keepdims=True)).astype(BF16)
        o = jnp.dot(p, vm, preferred_element_type=F32)
        out = jnp.where(sel, o, out)
    return out.astype(BF16)


def _memkv_kernel(mem_ref, g_ref, wkt_ref, wv_ref, kmt_ref, vm_ref):
    mn = _rms(mem_ref[...], g_ref[...]).astype(BF16)
    kmt = lax.dot_general(wkt_ref[...], mn, (((1,), (1,)), ((), ())),
                          preferred_element_type=F32)
    kmt_ref[...] = kmt.astype(BF16)
    vm_ref[...] = jnp.dot(mn, wv_ref[...], preferred_element_type=F32).astype(BF16)


def _memkv_call(mem, g_mem, w_kt, w_v):
    b, l, d = mem.shape
    depth = g_mem.shape[0]
    return pl.pallas_call(
        _memkv_kernel,
        grid=(b, depth),
        in_specs=[
            pl.BlockSpec((None, l, d), lambda bi, i: (bi, 0, 0)),
            pl.BlockSpec((None, 1, d), lambda bi, i: (i, 0, 0)),
            pl.BlockSpec((None, D_MEMH, d), lambda bi, i: (i, 0, 0)),
            pl.BlockSpec((None, d, D_MEMH), lambda bi, i: (i, 0, 0)),
        ],
        out_specs=[
            pl.BlockSpec((None, None, D_MEMH, l), lambda bi, i: (i, bi, 0, 0)),
            pl.BlockSpec((None, None, l, D_MEMH), lambda bi, i: (i, bi, 0, 0)),
        ],
        out_shape=[
            jax.ShapeDtypeStruct((depth, b, D_MEMH, l), BF16),
            jax.ShapeDtypeStruct((depth, b, l, D_MEMH), BF16),
        ],
        compiler_params=pltpu.CompilerParams(
            dimension_semantics=("arbitrary", "arbitrary"), vmem_limit_bytes=VMEM_LIMIT),
        name="memkv",
    )(mem, g_mem.reshape(depth, 1, d), w_kt, w_v)


def _const_spec(shape, layer):
    nd = len(shape)
    return pl.BlockSpec((None,) + tuple(shape), lambda bi, j: (layer,) + (0,) * nd,
                        pipeline_mode=pl.Buffered(1))


def _ffn_specs(d, d_ff, layer, which):
    g_idx = 3 * layer + (0 if which == 1 else 2)
    return [
        _const_spec((1, d), g_idx),
        _const_spec((d, d_ff), layer),
        _const_spec((d, d_ff), layer),
        _const_spec((d_ff, d), layer),
        _const_spec((1, d), g_idx),
    ]


def _head_kernel(*refs, mixer, n_ffn):
    it = iter(refs)
    h_ref = next(it)
    ffn_refs = [tuple(next(it) for _ in range(5)) for _ in range(n_ffn)]
    g_mix_ref = next(it)
    if mixer == "pool":
        w_in_ref = next(it)
        h_out_ref, tok_ref, qm_ref = it
    else:
        w_q_ref, w_kt_ref, w_v_ref, w_qm_ref = (next(it) for _ in range(4))
        h_out_ref, q_ref, kt_ref, v_ref, qm_ref = it
    slices = _row_slices(h_ref.shape[0])
    xs = [h_ref[rows] for rows in slices]
    for g_pre, wg, wu, wd, g_post in ffn_refs:
        xs = _ffn_half_step(xs, g_pre[...], wg, wu, wd, g_post[...])
    for rows, x in zip(slices, xs):
        h_out_ref[rows] = x
        u = _rms(x, g_mix_ref[...]).astype(BF16)
        if mixer == "pool":
            proj = jnp.dot(u, w_in_ref[...], preferred_element_type=F32)
            tok_ref[rows] = proj[:, :D_TOK]
            qm_ref[rows] = proj[:, D_TOK:].astype(BF16)
        else:
            q = jnp.dot(u, w_q_ref[...], preferred_element_type=F32)
            q_ref[rows] = (q * (HEAD_DIM ** -0.5)).astype(BF16)
            kt = lax.dot_general(w_kt_ref[...], u, (((1,), (1,)), ((), ())),
                                 preferred_element_type=F32)
            for g in range(D_TOK // SB_LANES):
                for t in range(kt.shape[1] // SB_TILE):
                    kt_ref[g, rows.start // SB_TILE + t] = kt[
                        g * SB_LANES:(g + 1) * SB_LANES,
                        t * SB_TILE:(t + 1) * SB_TILE].astype(BF16)
            v_ref[rows] = jnp.dot(u, w_v_ref[...], preferred_element_type=F32).astype(BF16)
            qm_ref[rows] = jnp.dot(u, w_qm_ref[...], preferred_element_type=F32).astype(BF16)


def _head_call(h, params, layer, mixer, ffn_list):
    b, s, d = h.shape
    tm = min(TOKEN_TILE, s)
    d_ff = params["ffn1_gate"].shape[2]
    tile = lambda w: pl.BlockSpec((None, tm, w), lambda bi, j: (bi, j, 0))
    in_specs = [tile(d)]
    args = [h]
    for (li, which) in ffn_list:
        in_specs += _ffn_specs(d, d_ff, li, which)
        pre = "ffn1" if which == 1 else "ffn2"
        args += [params["g_pre"], params[pre + "_gate"], params[pre + "_up"],
                 params[pre + "_down"], params["g_post"]]
    in_specs.append(_const_spec((1, d), 3 * layer + 1))
    args.append(params["g_pre"])
    j = layer // 2
    if mixer == "pool":
        in_specs.append(_const_spec((d, D_TOK + D_MEMH), j))
        args.append(params["w_in_pool"])
        out_specs = [tile(d), tile(D_TOK), tile(D_MEMH)]
        out_shape = [jax.ShapeDtypeStruct((b, s, d), F32),
                     jax.ShapeDtypeStruct((b, s, D_TOK), F32),
                     jax.ShapeDtypeStruct((b, s, D_MEMH), BF16)]
    else:
        in_specs += [_const_spec((d, D_TOK), j), _const_spec((D_TOK, d), j),
                     _const_spec((d, D_TOK), j), _const_spec((d, D_MEMH), j)]
        args += [params["w_sb_q"], params["w_sb_kt"], params["w_sb_v"], params["w_sb_qm"]]
        n_group = D_TOK // SB_LANES
        out_specs = [tile(d), tile(D_TOK),
                     pl.BlockSpec((None, n_group, tm // SB_TILE, SB_LANES, SB_TILE),
                                  lambda bi, j: (bi, 0, j, 0, 0)),
                     tile(D_TOK), tile(D_MEMH)]
        out_shape = [jax.ShapeDtypeStruct((b, s, d), F32),
                     jax.ShapeDtypeStruct((b, s, D_TOK), BF16),
                     jax.ShapeDtypeStruct((b, n_group, s // SB_TILE, SB_LANES, SB_TILE), BF16),
                     jax.ShapeDtypeStruct((b, s, D_TOK), BF16),
                     jax.ShapeDtypeStruct((b, s, D_MEMH), BF16)]
    return pl.pallas_call(
        functools.partial(_head_kernel, mixer=mixer, n_ffn=len(ffn_list)),
        grid=(b, s // tm),
        in_specs=in_specs, out_specs=out_specs, out_shape=out_shape,
        compiler_params=pltpu.CompilerParams(
            dimension_semantics=("arbitrary", "arbitrary"), vmem_limit_bytes=VMEM_LIMIT),
        name="head_" + mixer,
    )(*args)


def _pool_fill(u_ref, ext_ref, first_tile):
    t = u_ref.shape[0]

    @pl.when(first_tile)
    def _():
        ext_ref[0:POOL_HALO, :] = jnp.zeros((POOL_HALO, D_TOK), F32)

    @pl.when(jnp.logical_not(first_tile))
    def _():
        ext_ref[0:POOL_HALO, :] = ext_ref[t:t + POOL_HALO, :]

    ext_ref[POOL_HALO:POOL_HALO + t, :] = u_ref[...]


def _pool_mixer(ext_ref, rows, pw_ref, ps, tile_start):
    t = rows.stop - rows.start
    base = POOL_HALO + rows.start
    pos = tile_start + rows.start + lax.broadcasted_iota(jnp.int32, (t, POOL_GROUP), 0)
    outs = []
    for gi, w in enumerate(POOL_WINDOWS):
        cs = slice(gi * POOL_GROUP, (gi + 1) * POOL_GROUP)
        u = ext_ref[base:base + t, cs]
        win = u
        for back in range(1, w):
            win = win + ext_ref[base - back:base - back + t, cs]
        cnt = jnp.minimum(pos + 1, w).astype(F32)
        dlt = (win / cnt - u).astype(BF16)
        outs.append(jnp.dot(dlt, pw_ref[gi], preferred_element_type=F32))
    return (jnp.concatenate(outs, axis=-1) * ps).astype(BF16)


def _tail_kernel(*refs, mixer, n_ffn):
    it = iter(refs)
    h_ref, tok_ref, qm_ref, kmt_ref, vm_ref = (next(it) for _ in range(5))
    if mixer == "pool":
        pw_ref, ps_ref = next(it), next(it)
    w_out_ref, g_post_ref = next(it), next(it)
    ffn_refs = [tuple(next(it) for _ in range(5)) for _ in range(n_ffn)]
    h_out_ref = next(it)
    if mixer == "pool":
        ext_ref = next(it)
        j = pl.program_id(1)
        _pool_fill(tok_ref, ext_ref, j == 0)
    slices = _row_slices(h_ref.shape[0])
    xs = []
    for rows in slices:
        if mixer == "pool":
            tok = _pool_mixer(ext_ref, rows, pw_ref, ps_ref[...], j * tok_ref.shape[0])
        else:
            tok = tok_ref[rows]
        mo = _mem_attention(qm_ref[rows], kmt_ref, vm_ref)
        cat = jnp.concatenate([tok, mo], axis=-1)
        mix = jnp.dot(cat, w_out_ref[...], preferred_element_type=F32)
        xs.append(h_ref[rows] + _rms(mix, g_post_ref[...]))
    for g_pre, wg, wu, wd, g_post in ffn_refs:
        xs = _ffn_half_step(xs, g_pre[...], wg, wu, wd, g_post[...])
    for rows, x in zip(slices, xs):
        h_out_ref[rows] = x


def _tail_call(h, tok, qm, kmt, vm, params, layer, mixer, ffn_list):
    b, s, d = h.shape
    tm = min(TOKEN_TILE, s)
    d_ff = params["ffn1_gate"].shape[2]
    l = kmt.shape[-1]
    tile = lambda w: pl.BlockSpec((None, tm, w), lambda bi, j: (bi, j, 0))
    in_specs = [tile(d), tile(D_TOK), tile(D_MEMH),
                pl.BlockSpec((None, None, D_MEMH, l), lambda bi, j: (layer, bi, 0, 0)),
                pl.BlockSpec((None, None, l, D_MEMH), lambda bi, j: (layer, bi, 0, 0))]
    args = [h, tok, qm, kmt, vm]
    scratch = []
    if mixer == "pool":
        jl = layer // 2
        in_specs += [_const_spec((len(POOL_WINDOWS), POOL_GROUP, POOL_GROUP), jl),
                     _const_spec((1, D_TOK), jl)]
        args += [params["pool_w"], params["pool_scale"]]
        scratch = [pltpu.VMEM((POOL_HALO + tm, D_TOK), F32)]
    in_specs += [_const_spec((D_TOK + D_MEMH, d), layer), _const_spec((1, d), 3 * layer + 1)]
    args += [params["w_out"], params["g_post"]]
    for (li, which) in ffn_list:
        in_specs += _ffn_specs(d, d_ff, li, which)
        pre = "ffn1" if which == 1 else "ffn2"
        args += [params["g_pre"], params[pre + "_gate"], params[pre + "_up"],
                 params[pre + "_down"], params["g_post"]]
    return pl.pallas_call(
        functools.partial(_tail_kernel, mixer=mixer, n_ffn=len(ffn_list)),
        grid=(b, s // tm),
        in_specs=in_specs, out_specs=tile(d),
        out_shape=jax.ShapeDtypeStruct((b, s, d), F32),
        scratch_shapes=scratch,
        compiler_params=pltpu.CompilerParams(
            dimension_semantics=("arbitrary", "arbitrary"), vmem_limit_bytes=VMEM_LIMIT),
        name="tail_" + mixer,
    )(*args)


def _first_key_tile(qi):
    return (qi + 1) * (SB_TQ // SB_TILE) - 1


def _next_tile(qi, kj, wrap):
    qn = qi + wrap.astype(jnp.int32)
    return qn, jnp.where(wrap, _first_key_tile(qn), kj - 1)


def _sb_kernel(q_ref, kt_ref, v_ref, o_ref, *scratch):
    z_refs, ls_refs, hl_refs, a_refs = (scratch[2 * i:2 * i + 2] for i in range(4))
    acc_ref, later_ref, qexp_ref, vexp_ref = scratch[8:]
    tq, tk = SB_TQ, SB_TILE
    nk = kt_ref.shape[0]
    nq = nk * tk // tq
    n_stage = 4
    heads = range(SB_HEADS)

    row = lax.broadcasted_iota(jnp.int32, (tq, tk), 0)
    col = lax.broadcasted_iota(jnp.int32, (tq, tk), 1)
    ntri = jnp.where(lax.broadcasted_iota(jnp.int32, (tk, tk), 0)
                     >= lax.broadcasted_iota(jnp.int32, (tk, tk), 1),
                     -1.0, 0.0).astype(BF16)
    ntri2 = jnp.concatenate([ntri, ntri], axis=0)
    for ref in scratch[:10]:
        ref[...] = jnp.zeros(ref.shape, ref.dtype)
    for ref in z_refs:
        ref[...] = jnp.full(ref.shape, MASK_BIAS, F32)

    lane = lax.broadcasted_iota(jnp.int32, (tk, SB_LANES), 1)

    def expand_values(t, norms):
        k_sq, q_sq = norms
        vt = v_ref[pl.ds(pl.multiple_of(t * tk, tk), tk), :]
        for h in heads:
            own = (lane >= h * HEAD_DIM) & (lane < (h + 1) * HEAD_DIM)
            vexp_ref[t, h * tk:(h + 1) * tk, :] = jnp.where(own, vt, jnp.zeros_like(vt))
        ksq = jnp.sum(jnp.square(kt_ref[t].astype(F32)), axis=0, keepdims=True)
        qsq = jnp.sum(jnp.square(
            q_ref[pl.ds(pl.multiple_of(t * tk, tk), tk), :].astype(F32)), axis=1, keepdims=True)
        return jnp.maximum(k_sq, ksq), jnp.maximum(q_sq, qsq)

    k_sq, q_sq = lax.fori_loop(0, nk, expand_values,
                               (jnp.zeros((1, tk), F32), jnp.zeros((tk, 1), F32)))
    z_bound = jnp.sqrt(jnp.max(q_sq, axis=0, keepdims=True)
                       * jnp.max(k_sq, axis=1, keepdims=True)) * (LOG2E * 1.01)

    def expand_queries(qi):
        qt = q_ref[pl.ds(pl.multiple_of(qi * tq, tq), tq), :]
        lane_q = lax.broadcasted_iota(jnp.int32, (tq, SB_LANES), 1)
        for h in heads:
            own = (lane_q >= h * HEAD_DIM) & (lane_q < (h + 1) * HEAD_DIM)
            qexp_ref[h * tq:(h + 1) * tq] = jnp.where(own, qt, jnp.zeros_like(qt))

    def step(cur, tiles):
        (q0, k0), (q1, _), _, (q3, k3) = tiles
        prev = 1 - cur
        z_ref, z_in = z_refs[cur], z_refs[prev]
        ls_ref, ls_in = ls_refs[cur], ls_refs[prev]
        hl_ref, hl_in = hl_refs[cur], hl_refs[prev]
        a_ref, a_in = a_refs[cur], a_refs[prev]

        qc = jnp.minimum(q0, nq - 1)
        kc = jnp.minimum(k0, nk - 1)

        def stage_p():
            pv = jnp.dot(a_in[...], vexp_ref[k3], preferred_element_type=F32)
            acc = jnp.where(k3 == _first_key_tile(q3), pv, acc_ref[...] + pv)
            acc_ref[...] = acc
            o_ref[pl.ds(pl.multiple_of(q3 * tq, tq), tq), :] = acc.astype(BF16)

        def stage_l(rows):
            z = z_in[rows]
            nlf = jnp.where(z > SB_BIG, z, jnp.log2(1.0 + jnp.exp2(z)))
            later = later_ref[rows]
            ls_ref[rows] = z + later
            hi = nlf.astype(BF16)
            hl_ref[rows, 0:tk] = hi
            hl_ref[rows, tk:2 * tk] = (nlf - hi.astype(F32)).astype(BF16)
            later = later - jnp.sum(nlf, axis=-1, keepdims=True)
            later_ref[rows] = later
            return later

        stage_p()
        tops = []
        for g in range(0, SB_HEADS, SB_DOT_HEADS):
            grp = slice(g * tq, (g + SB_DOT_HEADS) * tq)
            zz = jnp.dot(qexp_ref[grp], kt_ref[kc], preferred_element_type=F32)
            ss = jnp.dot(hl_in[grp], ntri2, preferred_element_type=F32)
            for r in range(0, SB_DOT_HEADS * tq, SB_ROWS):
                rows = slice(g * tq + r, g * tq + r + SB_ROWS)
                tops.append(stage_l(rows))
                h, hr = g + r // tq, r % tq
                a_ref[hr:hr + SB_ROWS, h * tk:(h + 1) * tk] = jnp.exp2(
                    ls_in[rows] + ss[r:r + SB_ROWS]).astype(BF16)
                z_ref[rows] = zz[r:r + SB_ROWS] * LOG2E
        while len(tops) > 1:
            tops = [jnp.maximum(a, b) for a, b in zip(tops[0::2], tops[1::2])]
        dead = jnp.min(jnp.where(tops[0] + z_bound < -SB_SKIP, 1, 0)) == 1
        wrap = (k0 == 0) | (dead & (q0 == q1))

        @pl.when(kc * tk + tk > qc * tq)
        def _():
            bias = jnp.where(kc * tk + col < qc * tq + row, 0.0, MASK_BIAS)
            for h in heads:
                z_ref[h * tq:(h + 1) * tq] += bias

        @pl.when(kc == _first_key_tile(qc))
        def _():
            later_ref[...] = jnp.zeros(later_ref.shape, F32)

        @pl.when(wrap)
        def _():
            expand_queries(jnp.minimum(q0 + 1, nq - 1))

        return (_next_tile(q0, k0, wrap),) + tiles[:-1]

    expand_queries(0)
    start = (jnp.int32(0), jnp.int32(_first_key_tile(0)))
    tiles = lax.while_loop(lambda tiles: tiles[-2][0] < nq,
                           lambda tiles: step(1, step(0, tiles)), (start,) * n_stage)

    @pl.when(tiles[-1][0] < nq)
    def _():
        step(0, tiles)


def _sb_call(q, kt, v):
    b, s, _ = q.shape
    n_group = D_TOK // SB_LANES
    nk = s // SB_TILE
    tq = SB_TQ
    assert s % tq == 0 and tq % SB_TILE == 0
    rows_f32 = pltpu.VMEM((SB_HEADS * tq, SB_TILE), F32)
    return pl.pallas_call(
        _sb_kernel,
        grid=(b, n_group),
        in_specs=[
            pl.BlockSpec((None, s, SB_LANES), lambda bi, g: (bi, 0, g),
                         pipeline_mode=pl.Buffered(1)),
            pl.BlockSpec((None, None, nk, SB_LANES, SB_TILE), lambda bi, g: (bi, g, 0, 0, 0),
                         pipeline_mode=pl.Buffered(1)),
            pl.BlockSpec((None, s, SB_LANES), lambda bi, g: (bi, 0, g),
                         pipeline_mode=pl.Buffered(1)),
        ],
        out_specs=pl.BlockSpec((None, s, SB_LANES), lambda bi, g: (bi, 0, g)),
        out_shape=jax.ShapeDtypeStruct((b, s, D_TOK), BF16),
        scratch_shapes=2 * [rows_f32]
                       + 2 * [rows_f32]
                       + 2 * [pltpu.VMEM((SB_HEADS * tq, 2 * SB_TILE), BF16)]
                       + 2 * [pltpu.VMEM((tq, SB_HEADS * SB_TILE), BF16)]
                       + [pltpu.VMEM((tq, SB_LANES), F32),
                          pltpu.VMEM((SB_HEADS * tq, 1), F32),
                          pltpu.VMEM((SB_HEADS * tq, SB_LANES), BF16),
                          pltpu.VMEM((nk, SB_HEADS * SB_TILE, SB_LANES), BF16)],
        compiler_params=pltpu.CompilerParams(
            dimension_semantics=("arbitrary", "arbitrary"), vmem_limit_bytes=VMEM_LIMIT),
        name="stick_breaking",
    )(q, kt, v)


def kernel(x, mem, g_pre, g_post, g_mem, ffn1_gate, ffn1_up, ffn1_down, ffn2_gate, ffn2_up,
           ffn2_down, w_in_pool, pool_w, pool_scale, w_in_sb, w_mem_kv, w_out):
    depth, _, d = g_pre.shape
    assert x.shape[1] % max(SB_TILE, TOKEN_TILE) == 0 or x.shape[1] == SB_TILE
    params = {
        "g_pre": g_pre.reshape(depth * 3, 1, d),
        "g_post": g_post.reshape(depth * 3, 1, d),
        "ffn1_gate": ffn1_gate.astype(BF16), "ffn1_up": ffn1_up.astype(BF16),
        "ffn1_down": ffn1_down.astype(BF16),
        "ffn2_gate": ffn2_gate.astype(BF16), "ffn2_up": ffn2_up.astype(BF16),
        "ffn2_down": ffn2_down.astype(BF16),
        "w_in_pool": w_in_pool.astype(BF16),
        "pool_w": pool_w.astype(BF16),
        "pool_scale": pool_scale.reshape(pool_scale.shape[0], 1, D_TOK),
        "w_sb_q": w_in_sb[:, :, :D_TOK].astype(BF16),
        "w_sb_kt": jnp.swapaxes(w_in_sb[:, :, D_TOK:2 * D_TOK], 1, 2).astype(BF16),
        "w_sb_v": w_in_sb[:, :, 2 * D_TOK:3 * D_TOK].astype(BF16),
        "w_sb_qm": w_in_sb[:, :, 3 * D_TOK:].astype(BF16),
        "w_out": w_out.astype(BF16),
    }
    kmt, vm = _memkv_call(mem, g_mem,
                          jnp.swapaxes(w_mem_kv[:, :, :D_MEMH], 1, 2).astype(BF16),
                          w_mem_kv[:, :, D_MEMH:].astype(BF16))
    h = x
    for i in range(depth):
        mixer = "pool" if i % 2 == 0 else "sb"
        if mixer == "pool":
            h, tok, qm = _head_call(h, params, i, mixer, [(i, 1)])
        else:
            h, q, kt, v, qm = _head_call(h, params, i, mixer, [(i, 1)])
            tok = _sb_call(q, kt, v)
        h = _tail_call(h, tok, qm, kmt, vm, params, i, mixer, [(i, 2)])
    return h
```
